```python
import math
import jax, jax.numpy as jnp
from jax import lax
import numpy as np

D_MODEL = 4096
BATCH = 2
SEQ = 8192
DEPTH = 2

N_A = DEPTH // 2
N_B = DEPTH - N_A
N_EVEN = (DEPTH + 1) // 2
N_ODD = DEPTH // 2
POOL_WINDOWS = (2, 4, 8, 16)
N_POOL_GROUPS = 4
POOL_GROUP = D_MODEL // N_POOL_GROUPS
HEAD_DIM = 128
N_HEADS = D_MODEL // HEAD_DIM
MOBA_BLOCK = 256
MOBA_TOPK = 3
Q_CHUNK = 16
N_BUCKETS = 32
MAX_DISTANCE = 128
D_FF = (D_MODEL * 43) // 16
N_EXPERTS = 8
TOP_K = 2
D_FF_EXPERT = D_MODEL
EPS = 1e-6

kernel_name = 'yoco_pool_moba_moe_hybrid'


def rms_norm(x, g):
    xf = x.astype(jnp.float32)
    y = xf * lax.rsqrt(jnp.mean(xf * xf, axis=-1, keepdims=True) + EPS)
    return (y * g.astype(jnp.float32)).astype(x.dtype)


def adaln(c, w, b):
    return jnp.dot(jax.nn.silu(c), w) + b


def modulate(h, shift, scale):
    return h * (1 + scale[:, None, :]) + shift[:, None, :]


def t5_bucket(n):
    n = jnp.maximum(n, 0)
    max_exact = N_BUCKETS // 2
    nf = jnp.maximum(n, 1).astype(jnp.float32)
    large = max_exact + (jnp.log(nf / max_exact) / math.log(MAX_DISTANCE / max_exact)
                         * (N_BUCKETS - max_exact)).astype(jnp.int32)
    large = jnp.minimum(large, N_BUCKETS - 1)
    return jnp.where(n < max_exact, n, large)


def pool_mixer(h, w_pool, b_pool, ls):
    bsz, seq, d = h.shape
    hf = h.astype(jnp.float32)
    cs = jnp.concatenate([jnp.zeros((bsz, 1, d), jnp.float32), jnp.cumsum(hf, axis=1)], axis=1)
    t = jnp.arange(seq)
    groups = []
    for g, w in enumerate(POOL_WINDOWS):
        lo_c, hi_c = g * POOL_GROUP, (g + 1) * POOL_GROUP
        cs_g = cs[..., lo_c:hi_c]
        start = jnp.maximum(t + 1 - w, 0)
        win_sum = cs_g[:, 1:] - jnp.take(cs_g, start, axis=1)
        count = jnp.minimum(t + 1, w).astype(jnp.float32)[None, :, None]
        groups.append(win_sum / count - hf[..., lo_c:hi_c])
    p = jnp.stack(groups, axis=2).astype(h.dtype)
    y = jnp.einsum('bsgc,gcd->bsgd', p, w_pool) + b_pool
    return y.reshape(bsz, seq, d) * ls


def shared_kv(x, c, kv_ada_w, kv_ada_b, kv_norm_g, w_k, w_v, k_norm_g):
    bsz, seq, _ = x.shape
    n_blk = -(-seq // MOBA_BLOCK)
    pad = n_blk * MOBA_BLOCK - seq
    shift, scale = jnp.split(adaln(c, kv_ada_w, kv_ada_b), 2, axis=-1)
    h = modulate(rms_norm(x, kv_norm_g), shift, scale)
    k = rms_norm((h @ w_k).reshape(bsz, seq, N_HEADS, HEAD_DIM), k_norm_g)
    v = (h @ w_v).reshape(bsz, seq, N_HEADS, HEAD_DIM)

    def to_blocks(t):
        t = jnp.pad(t.transpose(0, 2, 1, 3), ((0, 0), (0, 0), (0, pad), (0, 0)))
        return t.reshape(bsz, N_HEADS, n_blk, MOBA_BLOCK, HEAD_DIM)

    k_blk = to_blocks(k)
    v_blk = to_blocks(v)
    k_mean = jnp.mean(k_blk.astype(jnp.float32), axis=3).astype(k_blk.dtype)
    return k_blk, v_blk, k_mean


def moba_attention(h, w_q, q_norm_g, w_o, k_blk, v_blk, k_mean, rel_bias):
    bsz, seq, _ = h.shape
    n_blk = k_blk.shape[2]
    topk = min(MOBA_TOPK, n_blk)
    q = rms_norm((h @ w_q).reshape(bsz, seq, N_HEADS, HEAD_DIM), q_norm_g)
    q = q.transpose(0, 2, 1, 3) * (HEAD_DIM ** -0.5)
    bias_t = rel_bias.T.astype(jnp.float32)
    b_ix = jnp.arange(bsz)[:, None, None, None]
    h_ix = jnp.arange(N_HEADS)[None, :, None, None]
    offs = jnp.arange(MOBA_BLOCK)

    def chunk(ci):
        t0 = ci * Q_CHUNK
        qc = lax.dynamic_slice_in_dim(q, t0, Q_CHUNK, axis=2)
        tq = t0 + jnp.arange(Q_CHUNK)
        own = t0 // MOBA_BLOCK
        gate = jnp.einsum('bhqe,bhne->bhqn', qc, k_mean).astype(jnp.float32)
        gate = jnp.where(jnp.arange(n_blk) < own, gate, -jnp.inf)
        _, idx = lax.top_k(gate, topk)
        slot_ok = jnp.arange(topk) < own
        k_sel = k_blk[b_ix, h_ix, idx]
        v_sel = v_blk[b_ix, h_ix, idx]
        s_past = jnp.einsum('bhqe,bhqkne->bhqkn', qc, k_sel).astype(jnp.float32)
        kpos = idx[..., None] * MOBA_BLOCK + offs
        s_past = s_past + bias_t[h_ix[..., None], t5_bucket(tq[:, None, None] - kpos)]
        s_past = jnp.where(slot_ok[:, None], s_past, -jnp.inf)
        k_own = lax.dynamic_index_in_dim(k_blk, own, axis=2, keepdims=False)
        v_own = lax.dynamic_index_in_dim(v_blk, own, axis=2, keepdims=False)
        rel = tq[:, None] - (own * MOBA_BLOCK + offs)[None, :]
        s_own = jnp.einsum('bhqe,bhne->bhqn', qc, k_own).astype(jnp.float32) + bias_t[:, t5_bucket(rel)][None]
        s_own = jnp.where(rel >= 0, s_own, -jnp.inf)
        n_past = topk * MOBA_BLOCK
        logits = jnp.concatenate([s_past.reshape(bsz, N_HEADS, Q_CHUNK, n_past), s_own], axis=-1)
        p = jax.nn.softmax(logits, axis=-1).astype(v_blk.dtype)
        p_past = p[..., :n_past].reshape(bsz, N_HEADS, Q_CHUNK, topk, MOBA_BLOCK)
        p_own = p[..., n_past:]
        return (jnp.einsum('bhqkn,bhqkne->bhqe', p_past, v_sel)
                + jnp.einsum('bhqn,bhne->bhqe', p_own, v_own))

    o = lax.map(chunk, jnp.arange(seq // Q_CHUNK))
    o = o.transpose(1, 0, 3, 2, 4).reshape(bsz, seq, N_HEADS * HEAD_DIM)
    return o @ w_o


def swiglu(h, w1, w3, w2):
    return (jax.nn.silu(h @ w1) * (h @ w3)) @ w2


def moe_swiglu(h, w_router, b_router, w1, w3, w2):
    logits = (h @ w_router).astype(jnp.float32) + b_router.astype(jnp.float32)
    top_val, top_idx = lax.top_k(logits, TOP_K)
    wts = jax.nn.softmax(top_val, axis=-1)
    combine = jnp.sum(jax.nn.one_hot(top_idx, N_EXPERTS, dtype=jnp.float32) * wts[..., None], axis=-2)
    y = jnp.zeros_like(h)
    for e in range(N_EXPERTS):
        y = y + combine[..., e:e + 1].astype(h.dtype) * swiglu(h, w1[e], w3[e], w2[e])
    return y


def setup_inputs(seed: int = 0) -> dict:
    key = jax.random.key(seed)
    ks = jax.random.split(key, 28)
    d = D_MODEL

    def nrm(k, shape, scale):
        return jax.random.normal(k, shape, jnp.float32) * scale

    return {
        'x': nrm(ks[0], (BATCH, SEQ, d), 1.0),
        'c': nrm(ks[1], (BATCH, d), 1.0),
        'ada_w': nrm(ks[2], (DEPTH, d, 6 * d), 0.5 * d ** -0.5),
        'ada_b': nrm(ks[3], (DEPTH, 6 * d), 0.01),
        'norm_mix_g': 1.0 + nrm(ks[4], (DEPTH, d), 0.05),
        'norm_ffn_g': 1.0 + nrm(ks[5], (DEPTH, d), 0.05),
        'pool_w': nrm(ks[6], (N_A, N_POOL_GROUPS, POOL_GROUP, POOL_GROUP), POOL_GROUP ** -0.5),
        'pool_b': nrm(ks[7], (N_A, N_POOL_GROUPS, POOL_GROUP), 0.01),
        'pool_scale': 0.5 + nrm(ks[8], (N_A, d), 0.1),
        'kv_ada_w': nrm(ks[9], (d, 2 * d), 0.5 * d ** -0.5),
        'kv_ada_b': nrm(ks[10], (2 * d,), 0.01),
        'kv_norm_g': 1.0 + nrm(ks[11], (d,), 0.05),
        'w_k': nrm(ks[12], (d, d), d ** -0.5),
        'w_v': nrm(ks[13], (d, d), d ** -0.5),
        'k_norm_g': 1.0 + nrm(ks[14], (HEAD_DIM,), 0.05),
        'w_q': nrm(ks[15], (N_B, d, d), d ** -0.5),
        'q_norm_g': 1.0 + nrm(ks[16], (N_B, HEAD_DIM), 0.05),
        'w_o': nrm(ks[17], (N_B, d, d), d ** -0.5),
        'rel_bias': nrm(ks[18], (N_BUCKETS, N_HEADS), 0.5),
        'ffn_w1': nrm(ks[19], (N_EVEN, d, D_FF), d ** -0.5),
        'ffn_w3': nrm(ks[20], (N_EVEN, d, D_FF), d ** -0.5),
        'ffn_w2': nrm(ks[21], (N_EVEN, D_FF, d), D_FF ** -0.5),
        'router_w': nrm(ks[22], (N_ODD, d, N_EXPERTS), d ** -0.5),
        'router_b': nrm(ks[23], (N_ODD, N_EXPERTS), 0.01),
        'moe_w1': nrm(ks[24], (N_ODD, N_EXPERTS, d, D_FF_EXPERT), d ** -0.5),
        'moe_w3': nrm(ks[25], (N_ODD, N_EXPERTS, d, D_FF_EXPERT), d ** -0.5),
        'moe_w2': nrm(ks[26], (N_ODD, N_EXPERTS, D_FF_EXPERT, d), D_FF_EXPERT ** -0.5),
    }


def reference(x, c, ada_w, ada_b, norm_mix_g, norm_ffn_g, pool_w, pool_b, pool_scale,
              kv_ada_w, kv_ada_b, kv_norm_g, w_k, w_v, k_norm_g,
              w_q, q_norm_g, w_o, rel_bias,
              ffn_w1, ffn_w3, ffn_w2, router_w, router_b, moe_w1, moe_w3, moe_w2):
    kv = None
    for layer in range(DEPTH):
        sh_m, sc_m, g_m, sh_f, sc_f, g_f = jnp.split(adaln(c, ada_w[layer], ada_b[layer]), 6, axis=-1)
        h = modulate(rms_norm(x, norm_mix_g[layer]), sh_m, sc_m)
        if layer < N_A:
            y = pool_mixer(h, pool_w[layer], pool_b[layer], pool_scale[layer])
        else:
            if layer == N_A:
                kv = shared_kv(x, c, kv_ada_w, kv_ada_b, kv_norm_g, w_k, w_v, k_norm_g)
            j = layer - N_A
            y = moba_attention(h, w_q[j], q_norm_g[j], w_o[j], kv[0], kv[1], kv[2], rel_bias)
        x = x + g_m[:, None, :] * y
        h = modulate(rms_norm(x, norm_ffn_g[layer]), sh_f, sc_f)
        if layer % 2 == 0:
            i = layer // 2
            y = swiglu(h, ffn_w1[i], ffn_w3[i], ffn_w2[i])
        else:
            i = layer // 2
            y = moe_swiglu(h, router_w[i], router_b[i], moe_w1[i], moe_w3[i], moe_w2[i])
        x = x + g_f[:, None, :] * y
    return x
```

```python
import functools
import math

import jax
import jax.numpy as jnp
from jax import lax
from jax.experimental import pallas as pl
from jax.experimental.pallas import tpu as pltpu

F32 = jnp.float32
BF16 = jnp.bfloat16

EPS = 1e-6
POOL_WINDOWS = (2, 4, 8, 16)
HEAD_DIM = 128
MOBA_BLOCK = 256
MOBA_TOPK = 3
MAX_DISTANCE = 128
TOP_K = 2

LANES = 128
SUBLANES = 8
POOL_HALO = 16
VMEM_LIMIT = 56 * 1024 * 1024


def _params(*sem):
    return pltpu.CompilerParams(dimension_semantics=sem, vmem_limit_bytes=VMEM_LIMIT)


def _tile(n, pref):
    t = min(n, pref)
    while n % t:
        t //= 2
    return t


def _rms_mod(x, g, shift, scale):
    ms = jnp.mean(x * x, axis=-1, keepdims=True)
    return (x * lax.rsqrt(ms + EPS) * g) * (1.0 + scale) + shift


def _adaln_kernel(c_ref, w_ref, b_ref, o_ref):
    c = c_ref[...]
    s = c * (1.0 / (1.0 + jnp.exp(-c)))
    o_ref[...] = jnp.dot(s, w_ref[...], preferred_element_type=F32,
                         precision=lax.Precision.HIGHEST) + b_ref[...]


def _adaln(c, w, b):
    n_layers, d, n = w.shape
    bsz = c.shape[0]
    assert bsz <= SUBLANES
    cp = jnp.pad(c, ((0, SUBLANES - bsz), (0, 0)))
    tn = _tile(n, 512)
    out = pl.pallas_call(
        _adaln_kernel,
        grid=(n_layers, n // tn),
        in_specs=[
            pl.BlockSpec((SUBLANES, d), lambda l, j: (0, 0)),
            pl.BlockSpec((None, d, tn), lambda l, j: (l, 0, j)),
            pl.BlockSpec((None, 1, tn), lambda l, j: (l, 0, j)),
        ],
        out_specs=pl.BlockSpec((None, SUBLANES, tn), lambda l, j: (l, 0, j)),
        out_shape=jax.ShapeDtypeStruct((n_layers, SUBLANES, n), F32),
        compiler_params=_params("parallel", "parallel"),
        name="adaln",
    )(cp, w, b.reshape(n_layers, 1, n))
    return out[:, :bsz, :]


def _pool_kernel(x_ref, halo_ref, sh_ref, sc_ref, gate_ref, ng_ref, w_ref, pb_ref, ls_ref,
                 o_ref, ext_ref, *, tm, group):
    i = pl.program_id(1)
    ng, sh, sc = ng_ref[...], sh_ref[...], sc_ref[...]
    x = x_ref[...]
    ext_ref[0:POOL_HALO, :] = jnp.where(i > 0, _rms_mod(halo_ref[...], ng, sh, sc), 0.0)
    ext_ref[POOL_HALO:, :] = _rms_mod(x, ng, sh, sc)
    pos = lax.broadcasted_iota(jnp.int32, (tm, 1), 0) + i * tm
    for g, win in enumerate(POOL_WINDOWS):
        cols = slice(g * group, (g + 1) * group)
        e = ext_ref[:, cols]
        s, span = e, 1
        while span < win:
            s = s + pltpu.roll(s, span, 0)
            span *= 2
        count = jnp.minimum(pos + 1, win).astype(F32)
        p = s[POOL_HALO:, :] / count - e[POOL_HALO:, :]
        y = jnp.dot(p.astype(BF16), w_ref[g], preferred_element_type=F32) + pb_ref[g]
        o_ref[:, cols] = x[:, cols] + gate_ref[:, cols] * (y * ls_ref[:, cols])


def _pool_layer(x, sh, sc, gate, norm_g, pool_w, pool_b, pool_scale):
    bsz, seq, d = x.shape
    n_groups = len(POOL_WINDOWS)
    group = d // n_groups
    tm = _tile(seq, 256)
    assert tm % POOL_HALO == 0 and max(POOL_WINDOWS) <= POOL_HALO
    vec = pl.BlockSpec((None, 1, d), lambda b, i: (b, 0, 0))
    const = pl.BlockSpec((1, d), lambda b, i: (0, 0))
    return pl.pallas_call(
        functools.partial(_pool_kernel, tm=tm, group=group),
        grid=(bsz, seq // tm),
        in_specs=[
            pl.BlockSpec((None, tm, d), lambda b, i: (b, i, 0)),
            pl.BlockSpec((None, POOL_HALO, d),
                         lambda b, i: (b, jnp.maximum(i * (tm // POOL_HALO) - 1, 0), 0)),
            vec, vec, vec, const,
            pl.BlockSpec((n_groups, group, group), lambda b, i: (0, 0, 0)),
            pl.BlockSpec((n_groups, 1, group), lambda b, i: (0, 0, 0)),
            const,
        ],
        out_specs=pl.BlockSpec((None, tm, d), lambda b, i: (b, i, 0)),
        out_shape=jax.ShapeDtypeStruct((bsz, seq, d), F32),
        scratch_shapes=[pltpu.VMEM((POOL_HALO + tm, d), F32)],
        compiler_params=_params("parallel", "parallel"),
        name="pool_mixer",
    )(x, x, sh, sc, gate, norm_g.reshape(1, d), pool_w.astype(BF16),
      pool_b.reshape(n_groups, 1, group), pool_scale.reshape(1, d))


def _silu_mul(a, b):
    return (a * (1.0 / (1.0 + jnp.exp(-a)))) * b


def _ffn_kernel(x_ref, sh_ref, sc_ref, gate_ref, ng_ref, w1_ref, w3_ref, w2_ref, o_ref, h_ref,
                *, n_chunks):
    f = pl.program_id(1)

    @pl.when(f == 0)
    def _():
        h_ref[...] = _rms_mod(x_ref[...], ng_ref[...], sh_ref[...], sc_ref[...]).astype(BF16)

    h = h_ref[...]
    a = jnp.dot(h, w1_ref[...], preferred_element_type=F32)
    b = jnp.dot(h, w3_ref[...], preferred_element_type=F32)
    g = _silu_mul(a, b).astype(BF16)
    d = o_ref.shape[1]
    cw = d // n_chunks
    for n in range(n_chunks):
        cols = slice(n * cw, (n + 1) * cw)
        y = jnp.dot(g, w2_ref[:, cols], preferred_element_type=F32)

        @pl.when(f == 0)
        def _():
            o_ref[:, cols] = y

        @pl.when(f > 0)
        def _():
            o_ref[:, cols] += y

    @pl.when(f == pl.num_programs(1) - 1)
    def _():
        o_ref[...] = x_ref[...] + gate_ref[...] * o_ref[...]


def _pad_ff(w1, w3, w2, mult):
    ff = w1.shape[-1]
    pad = (-ff) % mult
    if pad:
        w1 = jnp.pad(w1, ((0, 0), (0, pad)))
        w3 = jnp.pad(w3, ((0, 0), (0, pad)))
        w2 = jnp.pad(w2, ((0, pad), (0, 0)))
    return w1, w3, w2


def _ffn_layer(x, sh, sc, gate, norm_g, w1, w3, w2):
    bsz, seq, d = x.shape
    tf = 256
    w1, w3, w2 = _pad_ff(w1.astype(BF16), w3.astype(BF16), w2.astype(BF16), tf)
    ff = w1.shape[-1]
    tm = _tile(seq, 512)
    per_b = seq // tm
    vec = pl.BlockSpec((None, 1, d), lambda i, f: (i // per_b, 0, 0))
    out = pl.pallas_call(
        functools.partial(_ffn_kernel, n_chunks=max(1, d // 1024)),
        grid=(bsz * per_b, ff // tf),
        in_specs=[
            pl.BlockSpec((tm, d), lambda i, f: (i, 0), pipeline_mode=pl.Buffered(1)),
            vec, vec, vec,
            pl.BlockSpec((1, d), lambda i, f: (0, 0)),
            pl.BlockSpec((d, tf), lambda i, f: (0, f)),
            pl.BlockSpec((d, tf), lambda i, f: (0, f)),
            pl.BlockSpec((tf, d), lambda i, f: (f, 0)),
        ],
        out_specs=pl.BlockSpec((tm, d), lambda i, f: (i, 0)),
        out_shape=jax.ShapeDtypeStruct((bsz * seq, d), F32),
        scratch_shapes=[pltpu.VMEM((tm, d), BF16)],
        compiler_params=_params("parallel", "arbitrary"),
        name="ffn_dense",
    )(x.reshape(bsz * seq, d), sh, sc, gate, norm_g.reshape(1, d), w1, w3, w2)
    return out.reshape(bsz, seq, d)


def _kvq_kernel(x_ref, shk_ref, sck_ref, shq_ref, scq_ref, ngk_ref, ngq_ref, gk_ref, gq_ref,
                w_ref, o_ref, h_ref, *, nk):
    j = pl.program_id(1)

    @pl.when(j == 0)
    def _():
        x = x_ref[...]
        xn = x * lax.rsqrt(jnp.mean(x * x, axis=-1, keepdims=True) + EPS)
        h_ref[0] = ((xn * ngk_ref[...]) * (1.0 + sck_ref[...]) + shk_ref[...]).astype(BF16)
        h_ref[1] = ((xn * ngq_ref[...]) * (1.0 + scq_ref[...]) + shq_ref[...]).astype(BF16)

    kind = j // nk
    r = jnp.dot(h_ref[kind // 2], w_ref[...], preferred_element_type=F32)
    gvec = jnp.where(kind == 2, gq_ref[...] * (HEAD_DIM ** -0.5), gk_ref[...])
    for hh in range(o_ref.shape[0]):
        rh = r[:, hh * HEAD_DIM:(hh + 1) * HEAD_DIM]
        ms = jnp.mean(rh * rh, axis=-1, keepdims=True)
        normed = rh * lax.rsqrt(ms + EPS) * gvec
        o_ref[hh] = jnp.where(kind == 1, rh, normed).astype(BF16)


def _kvq_proj(x, sh_kv, sc_kv, sh_q, sc_q, ng_kv, ng_q, k_norm_g, q_norm_g, w_k, w_v, w_q):
    bsz, seq, d = x.shape
    n_heads = d // HEAD_DIM
    w = jnp.concatenate([w_k.astype(BF16), w_v.astype(BF16), w_q.astype(BF16)], axis=1)
    tm = _tile(seq, 512)
    tn = _tile(d, 1024)
    nk = d // tn
    per_b = seq // tm
    hpt = tn // HEAD_DIM
    vec = pl.BlockSpec((None, 1, d), lambda i, j: (i // per_b, 0, 0))
    const = pl.BlockSpec((1, d), lambda i, j: (0, 0))
    hvec = pl.BlockSpec((1, HEAD_DIM), lambda i, j: (0, 0))
    return pl.pallas_call(
        functools.partial(_kvq_kernel, nk=nk),
        grid=(bsz * per_b, 3 * nk),
        in_specs=[
            pl.BlockSpec((tm, d), lambda i, j: (i, 0), pipeline_mode=pl.Buffered(1)),
            vec, vec, vec, vec, const, const, hvec, hvec,
            pl.BlockSpec((d, tn), lambda i, j: (0, j)),
        ],
        out_specs=pl.BlockSpec((None, None, hpt, tm, HEAD_DIM),
                               lambda i, j: (j // nk, i // per_b, j % nk, i % per_b, 0)),
        out_shape=jax.ShapeDtypeStruct((3, bsz, n_heads, seq, HEAD_DIM), BF16),
        scratch_shapes=[pltpu.VMEM((2, tm, d), BF16)],
        compiler_params=_params("parallel", "arbitrary"),
        name="kvq_proj",
    )(x.reshape(bsz * seq, d), sh_kv, sc_kv, sh_q, sc_q, ng_kv.reshape(1, d), ng_q.reshape(1, d),
      k_norm_g.reshape(1, HEAD_DIM), q_norm_g.reshape(1, HEAD_DIM), w)


def _bias_kernel(rb_ref, o_ref, *, n_buckets):
    h = pl.program_id(0)
    blk = o_ref.shape[-1]
    row = lax.broadcasted_iota(jnp.int32, (blk, blk), 0)
    col = lax.broadcasted_iota(jnp.int32, (blk, blk), 1)
    max_exact = n_buckets // 2
    far = rb_ref[h, n_buckets - 1]
    for which in range(2):
        rel = row - col + which * blk
        n = jnp.maximum(rel, 0)
        nf = jnp.maximum(n, 1).astype(F32)
        large = max_exact + (jnp.log(nf / max_exact) / math.log(MAX_DISTANCE / max_exact)
                             * (n_buckets - max_exact)).astype(jnp.int32)
        large = jnp.minimum(large, n_buckets - 1)
        bucket = jnp.where(n < max_exact, n, large)
        t = jnp.zeros((blk, blk), F32)
        for k in range(n_buckets):
            t = jnp.where(bucket == k, rb_ref[h, k] - far, t)
        o_ref[which] = jnp.where(rel >= 0, t, -jnp.inf)


def _bias_tables(rel_bias):
    n_buckets, n_heads = rel_bias.shape
    return pl.pallas_call(
        functools.partial(_bias_kernel, n_buckets=n_buckets),
        grid=(n_heads,),
        in_specs=[pl.BlockSpec(memory_space=pltpu.SMEM)],
        out_specs=pl.BlockSpec((None, 2, MOBA_BLOCK, MOBA_BLOCK), lambda h: (h, 0, 0, 0)),
        out_shape=jax.ShapeDtypeStruct((n_heads, 2, MOBA_BLOCK, MOBA_BLOCK), F32),
        compiler_params=_params("parallel"),
        name="bias_tables",
    )(rel_bias.T.astype(F32))


def _attn_kernel(q_ref, k_ref, v_ref, tab_ref, o_ref, kmean_ref, m_ref, l_ref, acc_ref, *, nb):
    i = pl.program_id(2)
    blk = MOBA_BLOCK
    nt = (((1,), (1,)), ((), ()))

    @pl.when(i == 0)
    def _():
        for jb in range(nb):
            kb = k_ref[jb * blk:(jb + 1) * blk, :].astype(F32)
            kmean_ref[jb:jb + 1, :] = jnp.mean(kb, axis=0, keepdims=True)

    q = q_ref[...]
    gate = lax.dot_general(q.astype(F32), kmean_ref[...], nt, preferred_element_type=F32,
                           precision=lax.Precision.HIGHEST)
    col = lax.broadcasted_iota(jnp.int32, (blk, nb), 1)
    colf = col.astype(F32)
    past = col < i
    g = jnp.where(past, gate, -jnp.inf)
    chosen = jnp.zeros((blk, nb), F32)
    for _ in range(MOBA_TOPK):
        top = jnp.max(g, axis=1, keepdims=True)
        idx = jnp.min(jnp.where(g == top, colf, float(nb)), axis=1, keepdims=True)
        pick = colf == idx
        chosen = jnp.where(pick, 1.0, chosen)
        g = jnp.where(pick, -jnp.inf, g)
    chosen = jnp.where(past, chosen, 0.0)

    def block(j):
        start = pl.multiple_of(j * blk, blk)
        kj = k_ref[pl.ds(start, blk), :]
        vj = v_ref[pl.ds(start, blk), :]
        return lax.dot_general(q, kj, nt, preferred_element_type=F32), vj

    s, vj = block(i)
    s = s + tab_ref[0]
    m = jnp.max(s, axis=1, keepdims=True)
    p = jnp.exp(s - m)
    m_ref[...] = m
    l_ref[...] = jnp.sum(p, axis=1, keepdims=True)
    acc_ref[...] = jnp.dot(p.astype(BF16), vj, preferred_element_type=F32)

    def update(j, bias):
        s, vj = block(j)
        if bias is not None:
            s = s + bias
        sel = jnp.max(jnp.where(col == j, chosen, 0.0), axis=1, keepdims=True)
        s = jnp.where(sel > 0.0, s, -jnp.inf)
        m_old = m_ref[...]
        m_new = jnp.maximum(m_old, jnp.max(s, axis=1, keepdims=True))
        alpha = jnp.exp(m_old - m_new)
        p = jnp.exp(s - m_new)
        l_ref[...] = alpha * l_ref[...] + jnp.sum(p, axis=1, keepdims=True)
        acc_ref[...] = alpha * acc_ref[...] + jnp.dot(p.astype(BF16), vj,
                                                      preferred_element_type=F32)
        m_ref[...] = m_new

    @pl.when(i >= 1)
    def _():
        update(i - 1, tab_ref[1])

    def far_block(j, carry):
        update(j, None)
        return carry

    lax.fori_loop(0, i - 1, far_block, 0)
    o_ref[...] = (acc_ref[...] / l_ref[...]).astype(BF16)


def _moba_attention(kvq, tables):
    _, bsz, n_heads, seq, _ = kvq.shape
    nb = seq // MOBA_BLOCK
    assert seq % MOBA_BLOCK == 0
    full = lambda which: pl.BlockSpec((None, None, None, seq, HEAD_DIM),
                                      lambda b, h, i: (which, b, h, 0, 0))
    return pl.pallas_call(
        functools.partial(_attn_kernel, nb=nb),
        grid=(bsz, n_heads, nb),
        in_specs=[
            pl.BlockSpec((None, None, None, MOBA_BLOCK, HEAD_DIM), lambda b, h, i: (2, b, h, i, 0)),
            full(0), full(1),
            pl.BlockSpec((None, 2, MOBA_BLOCK, MOBA_BLOCK), lambda b, h, i: (h, 0, 0, 0)),
        ],
        out_specs=pl.BlockSpec((None, MOBA_BLOCK, HEAD_DIM), lambda b, h, i: (b, i, h)),
        out_shape=jax.ShapeDtypeStruct((bsz, seq, n_heads * HEAD_DIM), BF16),
        scratch_shapes=[
            pltpu.VMEM((nb, HEAD_DIM), F32),
            pltpu.VMEM((MOBA_BLOCK, 1), F32),
            pltpu.VMEM((MOBA_BLOCK, 1), F32),
            pltpu.VMEM((MOBA_BLOCK, HEAD_DIM), F32),
        ],
        compiler_params=_params("parallel", "parallel", "arbitrary"),
        name="moba_attention",
    )(kvq, kvq, kvq, tables)


def _oproj_kernel(a_ref, w_ref, x_ref, gate_ref, ng_ref, sh_ref, sc_ref, wr_ref,
                  x3_ref, h_ref, lg_ref, xs_ref, *, nj, tn):
    j = pl.program_id(1)
    y = jnp.dot(a_ref[...], w_ref[...], preferred_element_type=F32)
    x3 = x_ref[...] + gate_ref[...] * y
    x3_ref[...] = x3
    xs_ref[j] = x3

    @pl.when(j == nj - 1)
    def _():
        d = nj * tn
        ss = jnp.zeros((x3.shape[0], 1), F32)
        for jj in range(nj):
            xj = xs_ref[jj]
            ss = ss + jnp.sum(xj * xj, axis=-1, keepdims=True)
        inv = lax.rsqrt(ss / d + EPS)
        logits = jnp.zeros(lg_ref.shape, F32)
        for jj in range(nj):
            cols = slice(jj * tn, (jj + 1) * tn)
            hj = (xs_ref[jj] * inv * ng_ref[:, cols]) * (1.0 + sc_ref[:, cols]) + sh_ref[:, cols]
            hi = hj.astype(BF16)
            lo = (hj - hi.astype(F32)).astype(BF16)
            h_ref[:, cols] = hi
            w = wr_ref[cols, :]
            whi = w.astype(BF16)
            wlo = (w - whi.astype(F32)).astype(BF16)
            logits = (logits + jnp.dot(hi, whi, preferred_element_type=F32)
                      + (jnp.dot(hi, wlo, preferred_element_type=F32)
                         + jnp.dot(lo, whi, preferred_element_type=F32)))
        lg_ref[...] = logits


def _oproj_layer(attn, x, gate, norm_g, sh, sc, w_o, router_w):
    bsz, seq, d = x.shape
    n_exp = router_w.shape[1]
    assert n_exp <= LANES
    wr = jnp.pad(router_w, ((0, 0), (0, LANES - n_exp)))
    tm = _tile(seq, 512)
    tn = _tile(d, 512)
    nj = d // tn
    per_b = seq // tm
    tok = bsz * seq
    vec_t = pl.BlockSpec((None, 1, tn), lambda i, j: (i // per_b, 0, j))
    vec = pl.BlockSpec((None, 1, d), lambda i, j: (i // per_b, 0, 0))
    x3, h, logits = pl.pallas_call(
        functools.partial(_oproj_kernel, nj=nj, tn=tn),
        grid=(bsz * per_b, nj),
        in_specs=[
            pl.BlockSpec((tm, d), lambda i, j: (i, 0)),
            pl.BlockSpec((d, tn), lambda i, j: (0, j)),
            pl.BlockSpec((tm, tn), lambda i, j: (i, j)),
            vec_t,
            pl.BlockSpec((1, d), lambda i, j: (0, 0)),
            vec, vec,
            pl.BlockSpec((d, LANES), lambda i, j: (0, 0)),
        ],
        out_specs=[
            pl.BlockSpec((tm, tn), lambda i, j: (i, j)),
            pl.BlockSpec((tm, d), lambda i, j: (i, 0)),
            pl.BlockSpec((tm, LANES), lambda i, j: (i, 0)),
        ],
        out_shape=[
            jax.ShapeDtypeStruct((tok, d), F32),
            jax.ShapeDtypeStruct((tok, d), BF16),
            jax.ShapeDtypeStruct((tok, LANES), F32),
        ],
        scratch_shapes=[pltpu.VMEM((nj, tm, tn), F32)],
        compiler_params=_params("parallel", "arbitrary"),
        name="oproj_prenorm_router",
    )(attn.reshape(tok, d), w_o.astype(BF16), x.reshape(tok, d), gate, norm_g.reshape(1, d),
      sh, sc, wr)
    return x3, h, logits


ROUTE_E0, ROUTE_E1, ROUTE_W0, ROUTE_W1, ROUTE_R0, ROUTE_R1 = range(6)


def _route_kernel(lg_ref, rb_ref, route_ref, cnt_ref, carry_ref, *, n_exp):
    step = pl.program_id(0)
    tr = lg_ref.shape[0]

    @pl.when(step == 0)
    def _():
        carry_ref[...] = jnp.zeros_like(carry_ref)

    lane = lax.broadcasted_iota(jnp.int32, (tr, LANES), 1)
    lanef = lane.astype(F32)
    logits = jnp.where(lane < n_exp, lg_ref[...] + rb_ref[...], -jnp.inf)
    m0 = jnp.max(logits, axis=1, keepdims=True)
    i0 = jnp.min(jnp.where(logits == m0, lanef, float(LANES)), axis=1, keepdims=True)
    oh0 = lanef == i0
    rest = jnp.where(oh0, -jnp.inf, logits)
    m1 = jnp.max(rest, axis=1, keepdims=True)
    i1 = jnp.min(jnp.where(rest == m1, lanef, float(LANES)), axis=1, keepdims=True)
    oh1 = lanef == i1
    e1 = jnp.exp(m1 - m0)
    w0 = 1.0 / (1.0 + e1)
    w1 = e1 / (1.0 + e1)
    oh = jnp.where(oh0 | oh1, 1.0, 0.0)
    r = lax.broadcasted_iota(jnp.int32, (tr, tr), 0)
    c = lax.broadcasted_iota(jnp.int32, (tr, tr), 1)
    earlier = jnp.where(c < r, 1.0, 0.0).astype(BF16)
    base = carry_ref[...] + jnp.dot(earlier, oh.astype(BF16), preferred_element_type=F32)
    r0 = jnp.sum(jnp.where(oh0, base, 0.0), axis=1, keepdims=True)
    r1 = jnp.sum(jnp.where(oh1, base, 0.0), axis=1, keepdims=True)
    total = carry_ref[...] + jnp.sum(oh, axis=0, keepdims=True)
    carry_ref[...] = total
    cnt_ref[...] = jnp.broadcast_to(total, cnt_ref.shape)
    out = jnp.zeros((tr, LANES), F32)
    for slot, val in ((ROUTE_E0, i0), (ROUTE_E1, i1), (ROUTE_W0, w0),
                      (ROUTE_W1, w1), (ROUTE_R0, r0), (ROUTE_R1, r1)):
        out = jnp.where(lane == slot, val, out)
    route_ref[...] = out


def _route(logits, router_b):
    tok = logits.shape[0]
    n_exp = router_b.shape[0]
    tr = _tile(tok, 512)
    rb = jnp.pad(router_b.astype(F32), (0, LANES - n_exp)).reshape(1, LANES)
    return pl.pallas_call(
        functools.partial(_route_kernel, n_exp=n_exp),
        grid=(tok // tr,),
        in_specs=[pl.BlockSpec((tr, LANES), lambda s: (s, 0)),
                  pl.BlockSpec((1, LANES), lambda s: (0, 0))],
        out_specs=[pl.BlockSpec((tr, LANES), lambda s: (s, 0)),
                   pl.BlockSpec((SUBLANES, LANES), lambda s: (0, 0))],
        out_shape=[jax.ShapeDtypeStruct((tok, LANES), F32),
                   jax.ShapeDtypeStruct((SUBLANES, LANES), F32)],
        scratch_shapes=[pltpu.VMEM((1, LANES), F32)],
        compiler_params=_params("arbitrary"),
        name="route_top2",
    )(logits, rb)


def _row_copy_kernel(start_ref, idx_ref, src_ref, *rest, td, gather):
    dst_ref, sem = rest[-2:]
    base = pl.program_id(0) * td
    tok = pl.num_programs(0) * td

    def copy(r, k):
        pos = start_ref[idx_ref[k, r]] + idx_ref[TOP_K + k, r]
        t = base + r
        if gather:
            return pltpu.make_async_copy(src_ref.at[pos], dst_ref.at[k * tok + t], sem)
        return pltpu.make_async_copy(src_ref.at[t], dst_ref.at[pos], sem)

    def issue(r, carry):
        for k in range(TOP_K):
            copy(r, k).start()
        return carry

    def drain(r, carry):
        for k in range(TOP_K):
            copy(r, k).wait()
        return carry

    lax.fori_loop(0, td, issue, 0)
    lax.fori_loop(0, td, drain, 0)


def _row_copy(src3, idx, row_start, *, gather, rows_out):
    n_steps, _, td = idx.shape
    any_spec = pl.BlockSpec(memory_space=pl.ANY)
    out_shape = jax.ShapeDtypeStruct((rows_out,) + src3.shape[1:], src3.dtype)
    operands = [row_start, idx, src3]
    in_specs = [pl.BlockSpec((None, 2 * TOP_K, td), lambda s, start: (s, 0, 0),
                             memory_space=pltpu.SMEM), any_spec]
    aliases = {}
    if not gather:
        operands.append(jnp.zeros(out_shape.shape, out_shape.dtype))
        in_specs.append(any_spec)
        aliases = {3: 0}
    return pl.pallas_call(
        functools.partial(_row_copy_kernel, td=td, gather=gather),
        grid_spec=pltpu.PrefetchScalarGridSpec(
            num_scalar_prefetch=1,
            grid=(n_steps,),
            in_specs=in_specs,
            out_specs=any_spec,
            scratch_shapes=[pltpu.SemaphoreType.DMA(())],
        ),
        out_shape=out_shape,
        input_output_aliases=aliases,
        compiler_params=pltpu.CompilerParams(dimension_semantics=("arbitrary",),
                                             has_side_effects=True),
        name="moe_gather" if gather else "moe_dispatch",
    )(*operands)


def _moe_ffn_kernel(te_ref, na_ref, h_ref, w1_ref, w3_ref, w2_ref, o_ref, acc_ref, *, n_chunks):
    i = pl.program_id(0)
    f = pl.program_id(1)
    nf = pl.num_programs(1)
    active = i < na_ref[0]

    @pl.when(active)
    def _():
        h = h_ref[...]
        a = jnp.dot(h, w1_ref[...], preferred_element_type=F32)
        b = jnp.dot(h, w3_ref[...], preferred_element_type=F32)
        g = _silu_mul(a, b).astype(BF16)
        d = acc_ref.shape[1]
        cw = d // n_chunks
        for n in range(n_chunks):
            cols = slice(n * cw, (n + 1) * cw)
            y = jnp.dot(g, w2_ref[:, cols], preferred_element_type=F32)

            @pl.when(f == 0)
            def _():
                acc_ref[:, cols] = y

            @pl.when(f > 0)
            def _():
                acc_ref[:, cols] += y

        @pl.when(f == nf - 1)
        def _():
            o_ref[...] = acc_ref[...].astype(o_ref.dtype)

    @pl.when(jnp.logical_not(active) & (f == nf - 1))
    def _():
        o_ref[...] = jnp.zeros_like(o_ref)


def _moe_ffn(hs, tile_expert, n_active, w1, w3, w2, *, tm):
    rows, d = hs.shape
    ff = w1.shape[-1]
    tf = _tile(ff, 256)
    n_tiles = rows // tm

    def w13_map(i, f, te, na):
        return (te[i], 0, jnp.where(i < na[0], f, 0))

    def w2_map(i, f, te, na):
        return (te[i], jnp.where(i < na[0], f, 0), 0)

    return pl.pallas_call(
        functools.partial(_moe_ffn_kernel, n_chunks=max(1, d // 1024)),
        grid_spec=pltpu.PrefetchScalarGridSpec(
            num_scalar_prefetch=2,
            grid=(n_tiles, ff // tf),
            in_specs=[
                pl.BlockSpec((tm, d), lambda i, f, te, na: (i, 0)),
                pl.BlockSpec((None, d, tf), w13_map),
                pl.BlockSpec((None, d, tf), w13_map),
                pl.BlockSpec((None, tf, d), w2_map),
            ],
            out_specs=pl.BlockSpec((tm, d), lambda i, f, te, na: (i, 0)),
            scratch_shapes=[pltpu.VMEM((tm, d), F32)],
        ),
        out_shape=jax.ShapeDtypeStruct((rows, d), BF16),
        compiler_params=_params("parallel", "arbitrary"),
        name="moe_ffn",
    )(tile_expert, n_active, hs, w1, w3, w2)


def _combine_kernel(x_ref, y_ref, route_ref, gate_ref, o_ref):
    route = route_ref[...]
    lane = lax.broadcasted_iota(jnp.int32, route.shape, 1)
    w0 = jnp.sum(jnp.where(lane == ROUTE_W0, route, 0.0), axis=1, keepdims=True)
    w1 = jnp.sum(jnp.where(lane == ROUTE_W1, route, 0.0), axis=1, keepdims=True)
    y = w0 * y_ref[0].astype(F32) + w1 * y_ref[1].astype(F32)
    o_ref[...] = x_ref[...] + gate_ref[...] * y


def _combine(x3, y2, route, gate, *, seq):
    tok, d = x3.shape
    tm = _tile(seq, 256)
    per_b = seq // tm
    return pl.pallas_call(
        _combine_kernel,
        grid=(tok // tm,),
        in_specs=[
            pl.BlockSpec((tm, d), lambda i: (i, 0)),
            pl.BlockSpec((TOP_K, tm, d), lambda i: (0, i, 0)),
            pl.BlockSpec((tm, LANES), lambda i: (i, 0)),
            pl.BlockSpec((None, 1, d), lambda i: (i // per_b, 0, 0)),
        ],
        out_specs=pl.BlockSpec((tm, d), lambda i: (i, 0)),
        out_shape=jax.ShapeDtypeStruct((tok, d), F32),
        compiler_params=_params("parallel"),
        name="moe_combine",
    )(x3, y2, route, gate)


def _moe_layer(x3, h, logits, gate, router_b, w1, w3, w2, *, seq):
    tok, d = x3.shape
    n_exp = router_b.shape[0]
    slabs = d // LANES
    tm = 512
    route, counts = _route(logits, router_b)

    cnt = counts[0, :n_exp].astype(jnp.int32)
    tiles_per = (cnt + tm - 1) // tm
    tile_end = jnp.cumsum(tiles_per)
    row_start = (tile_end - tiles_per) * tm
    n_tiles = (TOP_K * tok) // tm + n_exp
    tile_expert = jnp.minimum(
        jnp.searchsorted(tile_end, jnp.arange(n_tiles, dtype=jnp.int32), side="right"),
        n_exp - 1).astype(jnp.int32)
    n_active = tile_end[-1:].astype(jnp.int32)

    td = _tile(tok, 512)
    idx = jnp.concatenate([route[:, ROUTE_E0:ROUTE_E1 + 1], route[:, ROUTE_R0:ROUTE_R1 + 1]],
                          axis=1).astype(jnp.int32)
    idx = idx.reshape(tok // td, td, 2 * TOP_K).transpose(0, 2, 1)

    rows = n_tiles * tm
    hs3 = _row_copy(h.reshape(tok, slabs, LANES), idx, row_start, gather=False, rows_out=rows)
    ys = _moe_ffn(hs3.reshape(rows, d), tile_expert, n_active,
                  w1.astype(BF16), w3.astype(BF16), w2.astype(BF16), tm=tm)
    y3 = _row_copy(ys.reshape(rows, slabs, LANES), idx, row_start, gather=True,
                   rows_out=TOP_K * tok)
    return _combine(x3, y3.reshape(TOP_K, tok, d), route, gate, seq=seq)


def kernel(x, c, ada_w, ada_b, norm_mix_g, norm_ffn_g, pool_w, pool_b, pool_scale, kv_ada_w,
           kv_ada_b, kv_norm_g, w_k, w_v, k_norm_g, w_q, q_norm_g, w_o, rel_bias, ffn_w1, ffn_w3,
           ffn_w2, router_w, router_b, moe_w1, moe_w3, moe_w2):
    bsz, seq, d = x.shape
    assert ada_w.shape[0] == 2 and pool_w.shape[0] == 1 and w_q.shape[0] == 1

    mod = _adaln(c, ada_w, ada_b)
    kv_mod = _adaln(c, kv_ada_w[None], kv_ada_b[None])[0]

    def vecs(m, n):
        return [m[:, None, k * d:(k + 1) * d] for k in range(n)]

    sh_m0, sc_m0, g_m0, sh_f0, sc_f0, g_f0 = vecs(mod[0], 6)
    sh_m1, sc_m1, g_m1, sh_f1, sc_f1, g_f1 = vecs(mod[1], 6)
    sh_kv, sc_kv = vecs(kv_mod, 2)

    x1 = _pool_layer(x, sh_m0, sc_m0, g_m0, norm_mix_g[0], pool_w[0], pool_b[0], pool_scale[0])
    x2 = _ffn_layer(x1, sh_f0, sc_f0, g_f0, norm_ffn_g[0], ffn_w1[0], ffn_w3[0], ffn_w2[0])

    kvq = _kvq_proj(x2, sh_kv, sc_kv, sh_m1, sc_m1, kv_norm_g, norm_mix_g[1], k_norm_g,
                    q_norm_g[0], w_k, w_v, w_q[0])
    attn = _moba_attention(kvq, _bias_tables(rel_bias))
    x3, h, logits = _oproj_layer(attn, x2, g_m1, norm_ffn_g[1], sh_f1, sc_f1, w_o[0], router_w[0])

    out = _moe_layer(x3, h, logits, g_f1, router_b[0], moe_w1[0], moe_w3[0], moe_w2[0], seq=seq)
    return out.reshape(bsz, seq, d)
```

```python
import functools
import math

import jax
import jax.numpy as jnp
from jax import lax
from jax.experimental import pallas as pl
from jax.experimental.pallas import tpu as pltpu

F32 = jnp.float32
BF16 = jnp.bfloat16

EPS = 1e-6
POOL_WINDOWS = (2, 4, 8, 16)
HEAD_DIM = 128
MOBA_BLOCK = 256
MOBA_TOPK = 3
MAX_DISTANCE = 128
TOP_K = 2
LOG2_E = math.log2(math.e)
Q_SCALE = HEAD_DIM ** -0.5 * LOG2_E

LANES = 128
SUBLANES = 8
POOL_HALO = 16
VMEM_LIMIT = 56 * 1024 * 1024


def _params(*sem):
    return pltpu.CompilerParams(dimension_semantics=sem, vmem_limit_bytes=VMEM_LIMIT)


def _tile(n, pref):
    t = min(n, pref)
    while n % t:
        t //= 2
    return t


def _rms_mod(x, g, shift, scale):
    ms = jnp.mean(x * x, axis=-1, keepdims=True)
    return (x * lax.rsqrt(ms + EPS) * g) * (1.0 + scale) + shift


def _adaln_kernel(c_ref, w_ref, b_ref, o_ref):
    c = c_ref[...]
    s = c * (1.0 / (1.0 + jnp.exp(-c)))
    o_ref[...] = jnp.dot(s, w_ref[...], preferred_element_type=F32,
                         precision=lax.Precision.HIGHEST) + b_ref[...]


def _adaln(c, w, b):
    n_layers, d, n = w.shape
    bsz = c.shape[0]
    assert bsz <= SUBLANES
    cp = jnp.pad(c, ((0, SUBLANES - bsz), (0, 0)))
    tn = _tile(n, 512)
    out = pl.pallas_call(
        _adaln_kernel,
        grid=(n_layers, n // tn),
        in_specs=[
            pl.BlockSpec((SUBLANES, d), lambda l, j: (0, 0)),
            pl.BlockSpec((None, d, tn), lambda l, j: (l, 0, j)),
            pl.BlockSpec((None, 1, tn), lambda l, j: (l, 0, j)),
        ],
        out_specs=pl.BlockSpec((None, SUBLANES, tn), lambda l, j: (l, 0, j)),
        out_shape=jax.ShapeDtypeStruct((n_layers, SUBLANES, n), F32),
        compiler_params=_params("parallel", "parallel"),
        name="adaln",
    )(cp, w, b.reshape(n_layers, 1, n))
    return out[:, :bsz, :]


def _pool_kernel(x_ref, halo_ref, sh_ref, sc_ref, gate_ref, ng_ref, w_ref, pb_ref, ls_ref,
                 o_ref, ext_ref, *, tm, group):
    i = pl.program_id(1)
    ng, sh, sc = ng_ref[...], sh_ref[...], sc_ref[...]
    x = x_ref[...]
    ext_ref[0:POOL_HALO, :] = jnp.where(i > 0, _rms_mod(halo_ref[...], ng, sh, sc), 0.0)
    ext_ref[POOL_HALO:, :] = _rms_mod(x, ng, sh, sc)
    pos = lax.broadcasted_iota(jnp.int32, (tm, 1), 0) + i * tm
    for g, win in enumerate(POOL_WINDOWS):
        cols = slice(g * group, (g + 1) * group)
        e = ext_ref[:, cols]
        s, span = e, 1
        while span < win:
            s = s + pltpu.roll(s, span, 0)
            span *= 2
        count = jnp.minimum(pos + 1, win).astype(F32)
        p = s[POOL_HALO:, :] / count - e[POOL_HALO:, :]
        y = jnp.dot(p.astype(BF16), w_ref[g], preferred_element_type=F32) + pb_ref[g]
        o_ref[:, cols] = x[:, cols] + gate_ref[:, cols] * (y * ls_ref[:, cols])


def _pool_layer(x, sh, sc, gate, norm_g, pool_w, pool_b, pool_scale):
    bsz, seq, d = x.shape
    n_groups = len(POOL_WINDOWS)
    group = d // n_groups
    tm = _tile(seq, 256)
    assert tm % POOL_HALO == 0 and max(POOL_WINDOWS) <= POOL_HALO
    vec = pl.BlockSpec((None, 1, d), lambda b, i: (b, 0, 0))
    const = pl.BlockSpec((1, d), lambda b, i: (0, 0))
    return pl.pallas_call(
        functools.partial(_pool_kernel, tm=tm, group=group),
        grid=(bsz, seq // tm),
        in_specs=[
            pl.BlockSpec((None, tm, d), lambda b, i: (b, i, 0)),
            pl.BlockSpec((None, POOL_HALO, d),
                         lambda b, i: (b, jnp.maximum(i * (tm // POOL_HALO) - 1, 0), 0)),
            vec, vec, vec, const,
            pl.BlockSpec((n_groups, group, group), lambda b, i: (0, 0, 0)),
            pl.BlockSpec((n_groups, 1, group), lambda b, i: (0, 0, 0)),
            const,
        ],
        out_specs=pl.BlockSpec((None, tm, d), lambda b, i: (b, i, 0)),
        out_shape=jax.ShapeDtypeStruct((bsz, seq, d), F32),
        scratch_shapes=[pltpu.VMEM((POOL_HALO + tm, d), F32)],
        compiler_params=_params("parallel", "parallel"),
        name="pool_mixer",
    )(x, x, sh, sc, gate, norm_g.reshape(1, d), pool_w.astype(BF16),
      pool_b.reshape(n_groups, 1, group), pool_scale.reshape(1, d))


def _silu_mul(a, b):
    return (a * (1.0 / (1.0 + jnp.exp(-a)))) * b


def _pack_gate_up(w1, w3, tn):
    pad = (-w1.shape[-1]) % tn
    lead = w1.shape[:-2]
    d = w1.shape[-2]

    def tiles(w):
        w = jnp.pad(w.astype(BF16), [(0, 0)] * (w.ndim - 1) + [(0, pad)])
        return w.reshape(lead + (d, -1, 1, tn))

    both = jnp.concatenate([tiles(w1), tiles(w3)], axis=-2)
    both = jnp.moveaxis(both, -3, -4)
    return both.reshape(both.shape[:-2] + (2 * tn,))


def _pack_down(w2, ff_padded, tn):
    pad = ff_padded - w2.shape[-2]
    w = jnp.pad(w2.astype(BF16), [(0, 0)] * (w2.ndim - 2) + [(0, pad), (0, 0)])
    w = w.reshape(w.shape[:-1] + (-1, tn))
    return jnp.moveaxis(w, -2, -3)


def _gateup_kernel(x_ref, sh_ref, sc_ref, ng_ref, w_ref, o_ref, h_ref, *, tn, rows):
    @pl.when(pl.program_id(1) == 0)
    def _():
        for r in range(0, h_ref.shape[0], rows):
            h_ref[r:r + rows, :] = _rms_mod(x_ref[r:r + rows, :], ng_ref[...], sh_ref[...],
                                            sc_ref[...]).astype(BF16)

    r = jnp.dot(h_ref[...], w_ref[...], preferred_element_type=F32)
    o_ref[...] = _silu_mul(r[:, :tn], r[:, tn:]).astype(BF16)


def _down_kernel(g_ref, w_ref, x_ref, gate_ref, o_ref):
    o_ref[...] = x_ref[...] + gate_ref[...] * jnp.dot(g_ref[...], w_ref[...],
                                                      preferred_element_type=F32)


def _ffn_layer(x, sh, sc, gate, norm_g, w1, w3, w2):
    bsz, seq, d = x.shape
    tok = bsz * seq
    tn = 256
    w13 = _pack_gate_up(w1, w3, tn)
    nf = w13.shape[0]
    ff = nf * tn
    tm = _tile(seq, 1024)
    per_b = seq // tm
    vec = pl.BlockSpec((None, 1, d), lambda i, j: (i // per_b, 0, 0))
    x2 = x.reshape(tok, d)
    hidden = pl.pallas_call(
        functools.partial(_gateup_kernel, tn=tn, rows=_tile(tm, 256)),
        grid=(tok // tm, nf),
        in_specs=[
            pl.BlockSpec((tm, d), lambda i, j: (i, 0), pipeline_mode=pl.Buffered(1)),
            vec, vec,
            pl.BlockSpec((1, d), lambda i, j: (0, 0)),
            pl.BlockSpec((None, d, 2 * tn), lambda i, j: (j, 0, 0)),
        ],
        out_specs=pl.BlockSpec((tm, tn), lambda i, j: (i, j)),
        out_shape=jax.ShapeDtypeStruct((tok, ff), BF16),
        scratch_shapes=[pltpu.VMEM((tm, d), BF16)],
        compiler_params=_params("parallel", "arbitrary"),
        name="ffn_gate_up",
    )(x2, sh, sc, norm_g.reshape(1, d), w13)

    tn2 = _tile(d, 512)
    tm2 = _tile(seq, 512)
    per_b2 = seq // tm2
    out = pl.pallas_call(
        _down_kernel,
        grid=(tok // tm2, d // tn2),
        in_specs=[
            pl.BlockSpec((tm2, ff), lambda i, j: (i, 0)),
            pl.BlockSpec((None, ff, tn2), lambda i, j: (j, 0, 0)),
            pl.BlockSpec((tm2, tn2), lambda i, j: (i, j)),
            pl.BlockSpec((None, 1, tn2), lambda i, j: (i // per_b2, 0, j)),
        ],
        out_specs=pl.BlockSpec((tm2, tn2), lambda i, j: (i, j)),
        out_shape=jax.ShapeDtypeStruct((tok, d), F32),
        compiler_params=_params("parallel", "arbitrary"),
        name="ffn_down",
    )(hidden, _pack_down(w2, ff, tn2), x2, gate)
    return out.reshape(bsz, seq, d)


def _kvq_kernel(x_ref, shk_ref, sck_ref, shq_ref, scq_ref, ngk_ref, ngq_ref, gk_ref, gq_ref,
                w_ref, o_ref, h_ref, *, nk):
    j = pl.program_id(1)

    @pl.when(j == 0)
    def _():
        x = x_ref[...]
        xn = x * lax.rsqrt(jnp.mean(x * x, axis=-1, keepdims=True) + EPS)
        h_ref[0] = ((xn * ngk_ref[...]) * (1.0 + sck_ref[...]) + shk_ref[...]).astype(BF16)
        h_ref[1] = ((xn * ngq_ref[...]) * (1.0 + scq_ref[...]) + shq_ref[...]).astype(BF16)

    kind = j // nk
    r = jnp.dot(h_ref[kind // 2], w_ref[...], preferred_element_type=F32)
    gvec = jnp.where(kind == 2, gq_ref[...] * Q_SCALE, gk_ref[...])
    for hh in range(o_ref.shape[0]):
        rh = r[:, hh * HEAD_DIM:(hh + 1) * HEAD_DIM]
        ms = jnp.mean(rh * rh, axis=-1, keepdims=True)
        normed = rh * lax.rsqrt(ms + EPS) * gvec
        o_ref[hh] = jnp.where(kind == 1, rh, normed).astype(BF16)


def _kvq_proj(x, sh_kv, sc_kv, sh_q, sc_q, ng_kv, ng_q, k_norm_g, q_norm_g, w_k, w_v, w_q):
    bsz, seq, d = x.shape
    n_heads = d // HEAD_DIM
    w = jnp.concatenate([w_k.astype(BF16), w_v.astype(BF16), w_q.astype(BF16)], axis=1)
    tm = _tile(seq, 512)
    tn = _tile(d, 1024)
    nk = d // tn
    per_b = seq // tm
    hpt = tn // HEAD_DIM
    vec = pl.BlockSpec((None, 1, d), lambda i, j: (i // per_b, 0, 0))
    const = pl.BlockSpec((1, d), lambda i, j: (0, 0))
    hvec = pl.BlockSpec((1, HEAD_DIM), lambda i, j: (0, 0))
    return pl.pallas_call(
        functools.partial(_kvq_kernel, nk=nk),
        grid=(bsz * per_b, 3 * nk),
        in_specs=[
            pl.BlockSpec((tm, d), lambda i, j: (i, 0), pipeline_mode=pl.Buffered(1)),
            vec, vec, vec, vec, const, const, hvec, hvec,
            pl.BlockSpec((d, tn), lambda i, j: (0, j)),
        ],
        out_specs=pl.BlockSpec((None, None, hpt, tm, HEAD_DIM),
                               lambda i, j: (j // nk, i // per_b, j % nk, i % per_b, 0)),
        out_shape=jax.ShapeDtypeStruct((3, bsz, n_heads, seq, HEAD_DIM), BF16),
        scratch_shapes=[pltpu.VMEM((2, tm, d), BF16)],
        compiler_params=_params("parallel", "arbitrary"),
        name="kvq_proj",
    )(x.reshape(bsz * seq, d), sh_kv, sc_kv, sh_q, sc_q, ng_kv.reshape(1, d), ng_q.reshape(1, d),
      k_norm_g.reshape(1, HEAD_DIM), q_norm_g.reshape(1, HEAD_DIM), w)


def _bias_kernel(rb_ref, o_ref, *, n_buckets):
    h = pl.program_id(0)
    blk = o_ref.shape[-1]
    key = lax.broadcasted_iota(jnp.int32, (2 * blk, blk), 0)
    qry = lax.broadcasted_iota(jnp.int32, (2 * blk, blk), 1)
    rel = qry - key + blk
    max_exact = n_buckets // 2
    far = rb_ref[h, n_buckets - 1]
    n = jnp.maximum(rel, 0)
    nf = jnp.maximum(n, 1).astype(F32)
    large = max_exact + (jnp.log(nf / max_exact) / math.log(MAX_DISTANCE / max_exact)
                         * (n_buckets - max_exact)).astype(jnp.int32)
    large = jnp.minimum(large, n_buckets - 1)
    bucket = jnp.where(n < max_exact, n, large)
    t = jnp.zeros((2 * blk, blk), F32)
    for k in range(n_buckets):
        t = jnp.where(bucket == k, (rb_ref[h, k] - far) * LOG2_E, t)
    o_ref[...] = jnp.where(rel >= 0, t, -jnp.inf)


def _bias_tables(rel_bias):
    n_buckets, n_heads = rel_bias.shape
    return pl.pallas_call(
        functools.partial(_bias_kernel, n_buckets=n_buckets),
        grid=(n_heads,),
        in_specs=[pl.BlockSpec(memory_space=pltpu.SMEM)],
        out_specs=pl.BlockSpec((None, 2 * MOBA_BLOCK, MOBA_BLOCK), lambda h: (h, 0, 0)),
        out_shape=jax.ShapeDtypeStruct((n_heads, 2 * MOBA_BLOCK, MOBA_BLOCK), F32),
        compiler_params=_params("parallel"),
        name="bias_tables",
    )(rel_bias.T.astype(F32))


def _attn_kernel(q_ref, k_ref, vt_ref, tab_ref, o_ref, kmean_ref, chosen_ref, m_ref, l_ref,
                 acc_ref, *, nb, heads):
    i = pl.program_id(2)
    blk = MOBA_BLOCK
    pair = 2 * blk
    nt = (((1,), (1,)), ((), ()))

    @pl.when(i == 0)
    def _():
        for hh in range(heads):
            for jb in range(nb):
                kb = k_ref[hh, jb * blk:(jb + 1) * blk, :].astype(F32)
                kmean_ref[hh, jb:jb + 1, :] = jnp.mean(kb, axis=0, keepdims=True)

    row = lax.broadcasted_iota(jnp.int32, (nb, blk), 0)
    rowf = row.astype(F32)
    past = row < i
    prev = pl.multiple_of(jnp.maximum(i - 1, 0) * blk, blk)
    own = pl.multiple_of(i * blk, blk)
    qs = [q_ref[hh] for hh in range(heads)]

    def partial_softmax(blocks):
        m = functools.reduce(jnp.maximum, [jnp.max(s, axis=0, keepdims=True) for s in blocks])
        m_safe = jnp.where(m == -jnp.inf, 0.0, m)
        ps = [jnp.exp2(s - m_safe) for s in blocks]
        total = functools.reduce(jnp.add, [jnp.sum(p, axis=0, keepdims=True) for p in ps])
        return m, total, jnp.concatenate([p.astype(BF16) for p in ps], axis=0)


    first_scores = []
    for hh in range(heads):
        q = qs[hh]
        gate = lax.dot_general(kmean_ref[hh], q.astype(F32), nt, preferred_element_type=F32,
                               precision=lax.Precision.HIGHEST)
        g = jnp.where(past, gate, -jnp.inf)
        chosen = jnp.zeros((nb, blk), F32)
        for _ in range(MOBA_TOPK):
            top = jnp.max(g, axis=0, keepdims=True)
            idx = jnp.min(jnp.where(g == top, rowf, float(nb)), axis=0, keepdims=True)
            pick = rowf == idx
            chosen = jnp.where(pick, 1.0, chosen)
            g = jnp.where(pick, -jnp.inf, g)
        chosen = jnp.where(past, chosen, 0.0)
        sel_prev = jnp.sum(jnp.where(row == i - 1, chosen, 0.0), axis=0, keepdims=True)
        chosen_ref[hh] = jnp.where(row >= i - 1, 0.0, chosen)
        kk = jnp.concatenate([k_ref[hh, pl.ds(prev, blk), :], k_ref[hh, pl.ds(own, blk), :]],
                             axis=0)
        s = lax.dot_general(kk, q, nt, preferred_element_type=F32) + tab_ref[hh]
        first_scores.append([jnp.where(sel_prev > 0.0, s[:blk, :], -jnp.inf), s[blk:, :]])
    first_parts = [partial_softmax(s) for s in first_scores]
    for hh in range(heads):
        m, l, p = first_parts[hh]
        vv = jnp.concatenate([vt_ref[hh, :, pl.ds(prev, blk)], vt_ref[hh, :, pl.ds(own, blk)]],
                             axis=1)
        m_ref[hh] = m
        l_ref[hh] = l
        acc_ref[hh] = jnp.dot(vv, p, preferred_element_type=F32)

    def far_scores(hh, j0):
        start = pl.multiple_of(j0 * blk, pair)
        s = lax.dot_general(k_ref[hh, pl.ds(start, pair), :], qs[hh], nt,
                            preferred_element_type=F32)
        return [jnp.where(chosen_ref[hh, pl.ds(j0 + n, 1), :] > 0.0,
                          s[n * blk:(n + 1) * blk, :], -jnp.inf) for n in range(2)]

    def far_quad(t, carry):
        groups = [(hh, 4 * t + half) for half in (0, 2) for hh in range(heads)]
        scores = [far_scores(hh, j0) for hh, j0 in groups]
        parts = [partial_softmax(s) for s in scores]
        outs = [jnp.dot(vt_ref[hh, :, pl.ds(pl.multiple_of(j0 * blk, pair), pair)], p,
                        preferred_element_type=F32)
                for (hh, j0), (_, _, p) in zip(groups, parts)]
        for hh in range(heads):
            mine = [n for n, (gh, _) in enumerate(groups) if gh == hh]
            m_old = m_ref[hh]
            m_new = m_old
            for n in mine:
                m_new = jnp.maximum(m_new, parts[n][0])
            w_old = jnp.exp2(m_old - m_new)
            l = w_old * l_ref[hh]
            acc = w_old * acc_ref[hh]
            for n in mine:
                w = jnp.exp2(parts[n][0] - m_new)
                l = l + w * parts[n][1]
                acc = acc + w * outs[n]
            m_ref[hh] = m_new
            l_ref[hh] = l
            acc_ref[hh] = acc
        return carry

    lax.fori_loop(0, (i + 2) // 4, far_quad, 0)
    for hh in range(heads):
        o_ref[:, hh * HEAD_DIM:(hh + 1) * HEAD_DIM] = (acc_ref[hh] / l_ref[hh]).T.astype(BF16)


def _moba_attention(kvq, tables):
    _, bsz, n_heads, seq, _ = kvq.shape
    nb = seq // MOBA_BLOCK
    assert seq % MOBA_BLOCK == 0 and nb % 4 == 0
    vt = jnp.swapaxes(kvq[1], -1, -2)
    heads = 2 if n_heads % 2 == 0 else 1
    return pl.pallas_call(
        functools.partial(_attn_kernel, nb=nb, heads=heads),
        grid=(bsz, n_heads // heads, nb),
        in_specs=[
            pl.BlockSpec((None, None, heads, MOBA_BLOCK, HEAD_DIM),
                         lambda b, h, i: (2, b, h, i, 0)),
            pl.BlockSpec((None, None, heads, seq, HEAD_DIM), lambda b, h, i: (0, b, h, 0, 0)),
            pl.BlockSpec((None, heads, HEAD_DIM, seq), lambda b, h, i: (b, h, 0, 0)),
            pl.BlockSpec((heads, 2 * MOBA_BLOCK, MOBA_BLOCK), lambda b, h, i: (h, 0, 0)),
        ],
        out_specs=pl.BlockSpec((None, MOBA_BLOCK, heads * HEAD_DIM), lambda b, h, i: (b, i, h)),
        out_shape=jax.ShapeDtypeStruct((bsz, seq, n_heads * HEAD_DIM), BF16),
        scratch_shapes=[
            pltpu.VMEM((heads, nb, HEAD_DIM), F32),
            pltpu.VMEM((heads, nb, MOBA_BLOCK), F32),
            pltpu.VMEM((heads, 1, MOBA_BLOCK), F32),
            pltpu.VMEM((heads, 1, MOBA_BLOCK), F32),
            pltpu.VMEM((heads, HEAD_DIM, MOBA_BLOCK), F32),
        ],
        compiler_params=_params("parallel", "parallel", "arbitrary"),
        name="moba_attention",
    )(kvq, kvq, vt, tables)


def _oproj_kernel(a_ref, w_ref, x_ref, gate_ref, ng_ref, sh_ref, sc_ref, wr_ref,
                  x3_ref, h_ref, lg_ref, xs_ref, *, nj, tn):
    j = pl.program_id(1)
    y = jnp.dot(a_ref[...], w_ref[...], preferred_element_type=F32)
    x3 = x_ref[...] + gate_ref[...] * y
    x3_ref[...] = x3
    xs_ref[j] = x3

    @pl.when(j == nj - 1)
    def _():
        d = nj * tn
        ss = jnp.zeros((x3.shape[0], 1), F32)
        for jj in range(nj):
            xj = xs_ref[jj]
            ss = ss + jnp.sum(xj * xj, axis=-1, keepdims=True)
        inv = lax.rsqrt(ss / d + EPS)
        logits = jnp.zeros(lg_ref.shape, F32)
        for jj in range(nj):
            cols = slice(jj * tn, (jj + 1) * tn)
            hj = (xs_ref[jj] * inv * ng_ref[:, cols]) * (1.0 + sc_ref[:, cols]) + sh_ref[:, cols]
            hi = hj.astype(BF16)
            lo = (hj - hi.astype(F32)).astype(BF16)
            h_ref[:, cols] = hi
            w = wr_ref[cols, :]
            whi = w.astype(BF16)
            wlo = (w - whi.astype(F32)).astype(BF16)
            logits = (logits + jnp.dot(hi, whi, preferred_element_type=F32)
                      + (jnp.dot(hi, wlo, preferred_element_type=F32)
                         + jnp.dot(lo, whi, preferred_element_type=F32)))
        lg_ref[...] = logits


def _oproj_layer(attn, x, gate, norm_g, sh, sc, w_o, router_w):
    bsz, seq, d = x.shape
    n_exp = router_w.shape[1]
    assert n_exp <= LANES
    wr = jnp.pad(router_w, ((0, 0), (0, LANES - n_exp)))
    tm = _tile(seq, 512)
    tn = _tile(d, 512)
    nj = d // tn
    per_b = seq // tm
    tok = bsz * seq
    vec_t = pl.BlockSpec((None, 1, tn), lambda i, j: (i // per_b, 0, j))
    vec = pl.BlockSpec((None, 1, d), lambda i, j: (i // per_b, 0, 0))
    x3, h, logits = pl.pallas_call(
        functools.partial(_oproj_kernel, nj=nj, tn=tn),
        grid=(bsz * per_b, nj),
        in_specs=[
            pl.BlockSpec((tm, d), lambda i, j: (i, 0)),
            pl.BlockSpec((d, tn), lambda i, j: (0, j)),
            pl.BlockSpec((tm, tn), lambda i, j: (i, j)),
            vec_t,
            pl.BlockSpec((1, d), lambda i, j: (0, 0)),
            vec, vec,
            pl.BlockSpec((d, LANES), lambda i, j: (0, 0)),
        ],
        out_specs=[
            pl.BlockSpec((tm, tn), lambda i, j: (i, j)),
            pl.BlockSpec((tm, d), lambda i, j: (i, 0)),
            pl.BlockSpec((tm, LANES), lambda i, j: (i, 0)),
        ],
        out_shape=[
            jax.ShapeDtypeStruct((tok, d), F32),
            jax.ShapeDtypeStruct((tok, d), BF16),
            jax.ShapeDtypeStruct((tok, LANES), F32),
        ],
        scratch_shapes=[pltpu.VMEM((nj, tm, tn), F32)],
        compiler_params=_params("parallel", "arbitrary"),
        name="oproj_prenorm_router",
    )(attn.reshape(tok, d), w_o.astype(BF16), x.reshape(tok, d), gate, norm_g.reshape(1, d),
      sh, sc, wr)
    return x3, h, logits


ROUTE_E0, ROUTE_E1, ROUTE_W0, ROUTE_W1, ROUTE_R0, ROUTE_R1 = range(6)


def _route_kernel(lg_ref, rb_ref, route_ref, cnt_ref, carry_ref, *, n_exp):
    step = pl.program_id(0)
    tr = lg_ref.shape[0]

    @pl.when(step == 0)
    def _():
        carry_ref[...] = jnp.zeros_like(carry_ref)

    lane = lax.broadcasted_iota(jnp.int32, (tr, LANES), 1)
    lanef = lane.astype(F32)
    logits = jnp.where(lane < n_exp, lg_ref[...] + rb_ref[...], -jnp.inf)
    m0 = jnp.max(logits, axis=1, keepdims=True)
    i0 = jnp.min(jnp.where(logits == m0, lanef, float(LANES)), axis=1, keepdims=True)
    oh0 = lanef == i0
    rest = jnp.where(oh0, -jnp.inf, logits)
    m1 = jnp.max(rest, axis=1, keepdims=True)
    i1 = jnp.min(jnp.where(rest == m1, lanef, float(LANES)), axis=1, keepdims=True)
    oh1 = lanef == i1
    e1 = jnp.exp(m1 - m0)
    w0 = 1.0 / (1.0 + e1)
    w1 = e1 / (1.0 + e1)
    oh = jnp.where(oh0 | oh1, 1.0, 0.0)
    r = lax.broadcasted_iota(jnp.int32, (tr, tr), 0)
    c = lax.broadcasted_iota(jnp.int32, (tr, tr), 1)
    earlier = jnp.where(c < r, 1.0, 0.0).astype(BF16)
    base = carry_ref[...] + jnp.dot(earlier, oh.astype(BF16), preferred_element_type=F32)
    r0 = jnp.sum(jnp.where(oh0, base, 0.0), axis=1, keepdims=True)
    r1 = jnp.sum(jnp.where(oh1, base, 0.0), axis=1, keepdims=True)
    total = carry_ref[...] + jnp.sum(oh, axis=0, keepdims=True)
    carry_ref[...] = total
    cnt_ref[...] = jnp.broadcast_to(total, cnt_ref.shape)
    out = jnp.zeros((tr, LANES), F32)
    for slot, val in ((ROUTE_E0, i0), (ROUTE_E1, i1), (ROUTE_W0, w0),
                      (ROUTE_W1, w1), (ROUTE_R0, r0), (ROUTE_R1, r1)):
        out = jnp.where(lane == slot, val, out)
    route_ref[...] = out


def _route(logits, router_b):
    tok = logits.shape[0]
    n_exp = router_b.shape[0]
    tr = _tile(tok, 512)
    rb = jnp.pad(router_b.astype(F32), (0, LANES - n_exp)).reshape(1, LANES)
    return pl.pallas_call(
        functools.partial(_route_kernel, n_exp=n_exp),
        grid=(tok // tr,),
        in_specs=[pl.BlockSpec((tr, LANES), lambda s: (s, 0)),
                  pl.BlockSpec((1, LANES), lambda s: (0, 0))],
        out_specs=[pl.BlockSpec((tr, LANES), lambda s: (s, 0)),
                   pl.BlockSpec((SUBLANES, LANES), lambda s: (0, 0))],
        out_shape=[jax.ShapeDtypeStruct((tok, LANES), F32),
                   jax.ShapeDtypeStruct((SUBLANES, LANES), F32)],
        scratch_shapes=[pltpu.VMEM((1, LANES), F32)],
        compiler_params=_params("arbitrary"),
        name="route_top2",
    )(logits, rb)


def _row_copy_kernel(start_ref, idx_ref, src_ref, *rest, td, gather):
    dst_ref, sem = rest[-2:]

    def copy(r, k):
        pos = start_ref[idx_ref[k, r]] + idx_ref[TOP_K + k, r]
        if gather:
            return pltpu.make_async_copy(src_ref.at[pos], dst_ref.at[k, r], sem)
        return pltpu.make_async_copy(src_ref.at[r], dst_ref.at[pos], sem)

    def issue(r, carry):
        for k in range(TOP_K):
            copy(r, k).start()
        return carry

    def drain(r, carry):
        for k in range(TOP_K):
            copy(r, k).wait()
        return carry

    lax.fori_loop(0, td, issue, 0)
    lax.fori_loop(0, td, drain, 0)


def _row_copy(src3, idx, row_start, *, gather, rows_out):
    n_steps, _, td = idx.shape
    slab = src3.shape[1:]
    any_spec = pl.BlockSpec(memory_space=pl.ANY)
    idx_spec = pl.BlockSpec((None, 2 * TOP_K, td), lambda s, start: (s, 0, 0),
                            memory_space=pltpu.SMEM)
    if gather:
        out_shape = jax.ShapeDtypeStruct((TOP_K, n_steps * td) + slab, src3.dtype)
        operands = [row_start, idx, src3]
        in_specs = [idx_spec, any_spec]
        out_spec = pl.BlockSpec((TOP_K, td) + slab, lambda s, start: (0, s, 0, 0))
        aliases = {}
    else:
        out_shape = jax.ShapeDtypeStruct((rows_out,) + slab, src3.dtype)
        operands = [row_start, idx, src3, jnp.zeros(out_shape.shape, out_shape.dtype)]
        in_specs = [idx_spec, pl.BlockSpec((td,) + slab, lambda s, start: (s, 0, 0)), any_spec]
        out_spec = any_spec
        aliases = {3: 0}
    return pl.pallas_call(
        functools.partial(_row_copy_kernel, td=td, gather=gather),
        grid_spec=pltpu.PrefetchScalarGridSpec(
            num_scalar_prefetch=1,
            grid=(n_steps,),
            in_specs=in_specs,
            out_specs=out_spec,
            scratch_shapes=[pltpu.SemaphoreType.DMA(())],
        ),
        out_shape=out_shape,
        input_output_aliases=aliases,
        compiler_params=pltpu.CompilerParams(dimension_semantics=("arbitrary",),
                                             vmem_limit_bytes=VMEM_LIMIT,
                                             has_side_effects=True),
        name="moe_gather" if gather else "moe_dispatch",
    )(*operands)


def _moe_gateup_kernel(te_ref, na_ref, h_ref, w_ref, o_ref, *, tn):
    active = pl.program_id(0) < na_ref[0]

    @pl.when(active)
    def _():
        r = jnp.dot(h_ref[...], w_ref[...], preferred_element_type=F32)
        o_ref[...] = _silu_mul(r[:, :tn], r[:, tn:]).astype(BF16)

    @pl.when(jnp.logical_not(active))
    def _():
        o_ref[...] = jnp.zeros_like(o_ref)


def _moe_down_kernel(te_ref, na_ref, g_ref, w_ref, o_ref):
    active = pl.program_id(0) < na_ref[0]

    @pl.when(active)
    def _():
        o_ref[...] = jnp.dot(g_ref[...], w_ref[...], preferred_element_type=F32).astype(BF16)

    @pl.when(jnp.logical_not(active))
    def _():
        o_ref[...] = jnp.zeros_like(o_ref)


def _moe_ffn(hs, tile_expert, n_active, w1, w3, w2, *, tm):
    rows, d = hs.shape
    tn = _tile(w1.shape[-1], 512)
    w13 = _pack_gate_up(w1, w3, tn)
    nf = w13.shape[1]
    ff = nf * tn
    tn2 = _tile(d, 1024)
    n_tiles = rows // tm

    def w_map(i, j, te, na):
        return (te[i], jnp.where(i < na[0], j, 0), 0, 0)

    hidden = pl.pallas_call(
        functools.partial(_moe_gateup_kernel, tn=tn),
        grid_spec=pltpu.PrefetchScalarGridSpec(
            num_scalar_prefetch=2,
            grid=(n_tiles, nf),
            in_specs=[
                pl.BlockSpec((tm, d), lambda i, j, te, na: (i, 0)),
                pl.BlockSpec((None, None, d, 2 * tn), w_map),
            ],
            out_specs=pl.BlockSpec((tm, tn), lambda i, j, te, na: (i, j)),
        ),
        out_shape=jax.ShapeDtypeStruct((rows, ff), BF16),
        compiler_params=_params("parallel", "arbitrary"),
        name="moe_gate_up",
    )(tile_expert, n_active, hs, w13)
    return pl.pallas_call(
        _moe_down_kernel,
        grid_spec=pltpu.PrefetchScalarGridSpec(
            num_scalar_prefetch=2,
            grid=(n_tiles, d // tn2),
            in_specs=[
                pl.BlockSpec((tm, ff), lambda i, j, te, na: (i, 0)),
                pl.BlockSpec((None, None, ff, tn2), w_map),
            ],
            out_specs=pl.BlockSpec((tm, tn2), lambda i, j, te, na: (i, j)),
        ),
        out_shape=jax.ShapeDtypeStruct((rows, d), BF16),
        compiler_params=_params("parallel", "arbitrary"),
        name="moe_down",
    )(tile_expert, n_active, hidden, _pack_down(w2, ff, tn2))


def _combine_kernel(x_ref, y_ref, route_ref, gate_ref, o_ref):
    route = route_ref[...]
    lane = lax.broadcasted_iota(jnp.int32, route.shape, 1)
    w0 = jnp.sum(jnp.where(lane == ROUTE_W0, route, 0.0), axis=1, keepdims=True)
    w1 = jnp.sum(jnp.where(lane == ROUTE_W1, route, 0.0), axis=1, keepdims=True)
    y = w0 * y_ref[0].astype(F32) + w1 * y_ref[1].astype(F32)
    o_ref[...] = x_ref[...] + gate_ref[...] * y


def _combine(x3, y2, route, gate, *, seq):
    tok, d = x3.shape
    tm = _tile(seq, 256)
    per_b = seq // tm
    return pl.pallas_call(
        _combine_kernel,
        grid=(tok // tm,),
        in_specs=[
            pl.BlockSpec((tm, d), lambda i: (i, 0)),
            pl.BlockSpec((TOP_K, tm, d), lambda i: (0, i, 0)),
            pl.BlockSpec((tm, LANES), lambda i: (i, 0)),
            pl.BlockSpec((None, 1, d), lambda i: (i // per_b, 0, 0)),
        ],
        out_specs=pl.BlockSpec((tm, d), lambda i: (i, 0)),
        out_shape=jax.ShapeDtypeStruct((tok, d), F32),
        compiler_params=_params("parallel"),
        name="moe_combine",
    )(x3, y2, route, gate)


def _moe_layer(x3, h, logits, gate, router_b, w1, w3, w2, *, seq):
    tok, d = x3.shape
    n_exp = router_b.shape[0]
    slabs = d // LANES
    tm = 512
    route, counts = _route(logits, router_b)

    cnt = counts[0, :n_exp].astype(jnp.int32)
    tiles_per = (cnt + tm - 1) // tm
    tile_end = jnp.cumsum(tiles_per)
    row_start = (tile_end - tiles_per) * tm
    n_tiles = (TOP_K * tok) // tm + n_exp
    tile_expert = jnp.minimum(
        jnp.searchsorted(tile_end, jnp.arange(n_tiles, dtype=jnp.int32), side="right"),
        n_exp - 1).astype(jnp.int32)
    n_active = tile_end[-1:].astype(jnp.int32)

    td = _tile(tok, 256)
    idx = jnp.concatenate([route[:, ROUTE_E0:ROUTE_E1 + 1], route[:, ROUTE_R0:ROUTE_R1 + 1]],
                          axis=1).astype(jnp.int32)
    idx = idx.reshape(tok // td, td, 2 * TOP_K).transpose(0, 2, 1)

    rows = n_tiles * tm
    hs3 = _row_copy(h.reshape(tok, slabs, LANES), idx, row_start, gather=False, rows_out=rows)
    ys = _moe_ffn(hs3.reshape(rows, d), tile_expert, n_active, w1, w3, w2, tm=tm)
    y4 = _row_copy(ys.reshape(rows, slabs, LANES), idx, row_start, gather=True, rows_out=None)
    return _combine(x3, y4.reshape(TOP_K, tok, d), route, gate, seq=seq)


def kernel(x, c, ada_w, ada_b, norm_mix_g, norm_ffn_g, pool_w, pool_b, pool_scale, kv_ada_w,
           kv_ada_b, kv_norm_g, w_k, w_v, k_norm_g, w_q, q_norm_g, w_o, rel_bias, ffn_w1, ffn_w3,
           ffn_w2, router_w, router_b, moe_w1, moe_w3, moe_w2):
    bsz, seq, d = x.shape
    assert ada_w.shape[0] == 2 and pool_w.shape[0] == 1 and w_q.shape[0] == 1

    mod = _adaln(c, ada_w, ada_b)
    kv_mod = _adaln(c, kv_ada_w[None], kv_ada_b[None])[0]

    def vecs(m, n):
        return [m[:, None, k * d:(k + 1) * d] for k in range(n)]

    sh_m0, sc_m0, g_m0, sh_f0, sc_f0, g_f0 = vecs(mod[0], 6)
    sh_m1, sc_m1, g_m1, sh_f1, sc_f1, g_f1 = vecs(mod[1], 6)
    sh_kv, sc_kv = vecs(kv_mod, 2)

    x1 = _pool_layer(x, sh_m0, sc_m0, g_m0, norm_mix_g[0], pool_w[0], pool_b[0], pool_scale[0])
    x2 = _ffn_layer(x1, sh_f0, sc_f0, g_f0, norm_ffn_g[0], ffn_w1[0], ffn_w3[0], ffn_w2[0])

    kvq = _kvq_proj(x2, sh_kv, sc_kv, sh_m1, sc_m1, kv_norm_g, norm_mix_g[1], k_norm_g,
                    q_norm_g[0], w_k, w_v, w_q[0])
    attn = _moba_attention(kvq, _bias_tables(rel_bias))
    x3, h, logits = _oproj_layer(attn, x2, g_m1, norm_ffn_g[1], sh_f1, sc_f1, w_o[0], router_w[0])

    out = _moe_layer(x3, h, logits, g_f1, router_b[0], moe_w1[0], moe_w3[0], moe_w2[0], seq=seq)
    return out.reshape(bsz, seq, d)
```

```python
import functools
import math

import jax
import jax.numpy as jnp
from jax import lax
from jax.experimental import pallas as pl
from jax.experimental.pallas import tpu as pltpu

F32 = jnp.float32
BF16 = jnp.bfloat16

EPS = 1e-6
POOL_WINDOWS = (2, 4, 8, 16)
HEAD_DIM = 128
MOBA_BLOCK = 256
MOBA_TOPK = 3
MAX_DISTANCE = 128
TOP_K = 2
LOG2_E = math.log2(math.e)
Q_SCALE = HEAD_DIM ** -0.5 * LOG2_E

LANES = 128
SUBLANES = 8
POOL_HALO = 16
VMEM_LIMIT = 56 * 1024 * 1024


def _params(*sem):
    return pltpu.CompilerParams(dimension_semantics=sem, vmem_limit_bytes=VMEM_LIMIT)


def _tile(n, pref):
    t = min(n, pref)
    while n % t:
        t //= 2
    return t


def _rms_mod(x, g, shift, scale):
    ms = jnp.mean(x * x, axis=-1, keepdims=True)
    return (x * lax.rsqrt(ms + EPS) * g) * (1.0 + scale) + shift


def _adaln_kernel(c_ref, w_ref, b_ref, o_ref):
    c = c_ref[...]
    s = c * (1.0 / (1.0 + jnp.exp(-c)))
    o_ref[...] = jnp.dot(s, w_ref[...], preferred_element_type=F32,
                         precision=lax.Precision.HIGHEST) + b_ref[...]


def _adaln(c, w, b):
    n_layers, d, n = w.shape
    bsz = c.shape[0]
    assert bsz <= SUBLANES
    cp = jnp.pad(c, ((0, SUBLANES - bsz), (0, 0)))
    tn = _tile(n, 512)
    out = pl.pallas_call(
        _adaln_kernel,
        grid=(n_layers, n // tn),
        in_specs=[
            pl.BlockSpec((SUBLANES, d), lambda l, j: (0, 0)),
            pl.BlockSpec((None, d, tn), lambda l, j: (l, 0, j)),
            pl.BlockSpec((None, 1, tn), lambda l, j: (l, 0, j)),
        ],
        out_specs=pl.BlockSpec((None, SUBLANES, tn), lambda l, j: (l, 0, j)),
        out_shape=jax.ShapeDtypeStruct((n_layers, SUBLANES, n), F32),
        compiler_params=_params("parallel", "parallel"),
        name="adaln",
    )(cp, w, b.reshape(n_layers, 1, n))
    return out[:, :bsz, :]


def _pool_kernel(x_ref, halo_ref, sh_ref, sc_ref, gate_ref, ng_ref, w_ref, pb_ref, ls_ref,
                 o_ref, ext_ref, *, tm, group):
    i = pl.program_id(1)
    ng, sh, sc = ng_ref[...], sh_ref[...], sc_ref[...]
    x = x_ref[...]
    ext_ref[0:POOL_HALO, :] = jnp.where(i > 0, _rms_mod(halo_ref[...], ng, sh, sc), 0.0)
    ext_ref[POOL_HALO:, :] = _rms_mod(x, ng, sh, sc)
    pos = lax.broadcasted_iota(jnp.int32, (tm, 1), 0) + i * tm
    for g, win in enumerate(POOL_WINDOWS):
        cols = slice(g * group, (g + 1) * group)
        e = ext_ref[:, cols]
        s, span = e, 1
        while span < win:
            s = s + pltpu.roll(s, span, 0)
            span *= 2
        count = jnp.minimum(pos + 1, win).astype(F32)
        p = s[POOL_HALO:, :] / count - e[POOL_HALO:, :]
        y = jnp.dot(p.astype(BF16), w_ref[g], preferred_element_type=F32) + pb_ref[g]
        o_ref[:, cols] = x[:, cols] + gate_ref[:, cols] * (y * ls_ref[:, cols])


def _pool_layer(x, sh, sc, gate, norm_g, pool_w, pool_b, pool_scale):
    bsz, seq, d = x.shape
    n_groups = len(POOL_WINDOWS)
    group = d // n_groups
    tm = _tile(seq, 256)
    assert tm % POOL_HALO == 0 and max(POOL_WINDOWS) <= POOL_HALO
    vec = pl.BlockSpec((None, 1, d), lambda b, i: (b, 0, 0))
    const = pl.BlockSpec((1, d), lambda b, i: (0, 0))
    return pl.pallas_call(
        functools.partial(_pool_kernel, tm=tm, group=group),
        grid=(bsz, seq // tm),
        in_specs=[
            pl.BlockSpec((None, tm, d), lambda b, i: (b, i, 0)),
            pl.BlockSpec((None, POOL_HALO, d),
                         lambda b, i: (b, jnp.maximum(i * (tm // POOL_HALO) - 1, 0), 0)),
            vec, vec, vec, const,
            pl.BlockSpec((n_groups, group, group), lambda b, i: (0, 0, 0)),
            pl.BlockSpec((n_groups, 1, group), lambda b, i: (0, 0, 0)),
            const,
        ],
        out_specs=pl.BlockSpec((None, tm, d), lambda b, i: (b, i, 0)),
        out_shape=jax.ShapeDtypeStruct((bsz, seq, d), F32),
        scratch_shapes=[pltpu.VMEM((POOL_HALO + tm, d), F32)],
        compiler_params=_params("parallel", "parallel"),
        name="pool_mixer",
    )(x, x, sh, sc, gate, norm_g.reshape(1, d), pool_w.astype(BF16),
      pool_b.reshape(n_groups, 1, group), pool_scale.reshape(1, d))


def _silu_mul(a, b):
    return (a * (1.0 / (1.0 + jnp.exp(-a)))) * b


def _swiglu_weights(w1, w3, w2, tn):
    pad = (-w1.shape[-1]) % tn
    cols = [(0, 0)] * (w1.ndim - 1) + [(0, pad)]
    rows = [(0, 0)] * (w2.ndim - 2) + [(0, pad), (0, 0)]
    w1, w3, w2 = w1.astype(BF16), w3.astype(BF16), w2.astype(BF16)
    if pad:
        w1, w3, w2 = jnp.pad(w1, cols), jnp.pad(w3, cols), jnp.pad(w2, rows)
    return w1, w3, w2


def _gateup_kernel(x_ref, sh_ref, sc_ref, ng_ref, w1_ref, w3_ref, o_ref, h_ref, *, rows):
    @pl.when(pl.program_id(1) == 0)
    def _():
        for r in range(0, h_ref.shape[0], rows):
            h_ref[r:r + rows, :] = _rms_mod(x_ref[r:r + rows, :], ng_ref[...], sh_ref[...],
                                            sc_ref[...]).astype(BF16)

    h = h_ref[...]
    a = jnp.dot(h, w1_ref[...], preferred_element_type=F32)
    b = jnp.dot(h, w3_ref[...], preferred_element_type=F32)
    o_ref[...] = _silu_mul(a, b).astype(BF16)


def _down_kernel(g_ref, w_ref, x_ref, gate_ref, o_ref):
    o_ref[...] = x_ref[...] + gate_ref[...] * jnp.dot(g_ref[...], w_ref[...],
                                                      preferred_element_type=F32)


def _ffn_layer(x, sh, sc, gate, norm_g, w1, w3, w2):
    bsz, seq, d = x.shape
    tok = bsz * seq
    tn = 256
    w1, w3, w2 = _swiglu_weights(w1, w3, w2, tn)
    ff = w1.shape[-1]
    tm = _tile(seq, 1024)
    per_b = seq // tm
    vec = pl.BlockSpec((None, 1, d), lambda i, j: (i // per_b, 0, 0))
    w_up = pl.BlockSpec((d, tn), lambda i, j: (0, j))
    x2 = x.reshape(tok, d)
    hidden = pl.pallas_call(
        functools.partial(_gateup_kernel, rows=_tile(tm, 256)),
        grid=(tok // tm, ff // tn),
        in_specs=[
            pl.BlockSpec((tm, d), lambda i, j: (i, 0), pipeline_mode=pl.Buffered(1)),
            vec, vec,
            pl.BlockSpec((1, d), lambda i, j: (0, 0)),
            w_up, w_up,
        ],
        out_specs=pl.BlockSpec((tm, tn), lambda i, j: (i, j)),
        out_shape=jax.ShapeDtypeStruct((tok, ff), BF16),
        scratch_shapes=[pltpu.VMEM((tm, d), BF16)],
        compiler_params=_params("parallel", "arbitrary"),
        name="ffn_gate_up",
    )(x2, sh, sc, norm_g.reshape(1, d), w1, w3)

    tn2 = _tile(d, 512)
    tm2 = _tile(seq, 512)
    per_b2 = seq // tm2
    out = pl.pallas_call(
        _down_kernel,
        grid=(tok // tm2, d // tn2),
        in_specs=[
            pl.BlockSpec((tm2, ff), lambda i, j: (i, 0)),
            pl.BlockSpec((ff, tn2), lambda i, j: (0, j)),
            pl.BlockSpec((tm2, tn2), lambda i, j: (i, j)),
            pl.BlockSpec((None, 1, tn2), lambda i, j: (i // per_b2, 0, j)),
        ],
        out_specs=pl.BlockSpec((tm2, tn2), lambda i, j: (i, j)),
        out_shape=jax.ShapeDtypeStruct((tok, d), F32),
        compiler_params=_params("parallel", "arbitrary"),
        name="ffn_down",
    )(hidden, w2, x2, gate)
    return out.reshape(bsz, seq, d)


def _kvq_kernel(x_ref, shk_ref, sck_ref, shq_ref, scq_ref, ngk_ref, ngq_ref, gk_ref, gq_ref,
                w_ref, o_ref, h_ref, *, nk):
    j = pl.program_id(1)

    @pl.when(j == 0)
    def _():
        rows = _tile(x_ref.shape[0], 128)
        for r in range(0, x_ref.shape[0], rows):
            x = x_ref[r:r + rows, :]
            xn = x * lax.rsqrt(jnp.mean(x * x, axis=-1, keepdims=True) + EPS)
            h_ref[0, r:r + rows, :] = ((xn * ngk_ref[...]) * (1.0 + sck_ref[...])
                                       + shk_ref[...]).astype(BF16)
            h_ref[1, r:r + rows, :] = ((xn * ngq_ref[...]) * (1.0 + scq_ref[...])
                                       + shq_ref[...]).astype(BF16)

    kind = j // nk
    r = jnp.dot(h_ref[kind // 2], w_ref[...], preferred_element_type=F32)
    gvec = jnp.where(kind == 2, gq_ref[...] * Q_SCALE, gk_ref[...])
    for hh in range(o_ref.shape[0]):
        rh = r[:, hh * HEAD_DIM:(hh + 1) * HEAD_DIM]
        ms = jnp.mean(rh * rh, axis=-1, keepdims=True)
        normed = rh * lax.rsqrt(ms + EPS) * gvec
        o_ref[hh] = jnp.where(kind == 1, rh, normed).astype(BF16)


def _kvq_proj(x, sh_kv, sc_kv, sh_q, sc_q, ng_kv, ng_q, k_norm_g, q_norm_g, w_k, w_v, w_q):
    bsz, seq, d = x.shape
    n_heads = d // HEAD_DIM
    w = jnp.concatenate([w_k.astype(BF16), w_v.astype(BF16), w_q.astype(BF16)], axis=1)
    tm = _tile(seq, 512)
    tn = _tile(d, 1024)
    nk = d // tn
    per_b = seq // tm
    hpt = tn // HEAD_DIM
    vec = pl.BlockSpec((None, 1, d), lambda i, j: (i // per_b, 0, 0))
    const = pl.BlockSpec((1, d), lambda i, j: (0, 0))
    hvec = pl.BlockSpec((1, HEAD_DIM), lambda i, j: (0, 0))
    return pl.pallas_call(
        functools.partial(_kvq_kernel, nk=nk),
        grid=(bsz * per_b, 3 * nk),
        in_specs=[
            pl.BlockSpec((tm, d), lambda i, j: (i, 0), pipeline_mode=pl.Buffered(1)),
            vec, vec, vec, vec, const, const, hvec, hvec,
            pl.BlockSpec((d, tn), lambda i, j: (0, j)),
        ],
        out_specs=pl.BlockSpec((None, None, hpt, tm, HEAD_DIM),
                               lambda i, j: (j // nk, i // per_b, j % nk, i % per_b, 0)),
        out_shape=jax.ShapeDtypeStruct((3, bsz, n_heads, seq, HEAD_DIM), BF16),
        scratch_shapes=[pltpu.VMEM((2, tm, d), BF16)],
        compiler_params=_params("parallel", "arbitrary"),
        name="kvq_proj",
    )(x.reshape(bsz * seq, d), sh_kv, sc_kv, sh_q, sc_q, ng_kv.reshape(1, d), ng_q.reshape(1, d),
      k_norm_g.reshape(1, HEAD_DIM), q_norm_g.reshape(1, HEAD_DIM), w)


def _bias_kernel(rb_ref, o_ref, *, n_buckets):
    h = pl.program_id(0)
    blk = o_ref.shape[-1]
    key = lax.broadcasted_iota(jnp.int32, (2 * blk, blk), 0)
    qry = lax.broadcasted_iota(jnp.int32, (2 * blk, blk), 1)
    rel = qry - key + blk
    max_exact = n_buckets // 2
    far = rb_ref[h, n_buckets - 1]
    n = jnp.maximum(rel, 0)
    nf = jnp.maximum(n, 1).astype(F32)
    large = max_exact + (jnp.log(nf / max_exact) / math.log(MAX_DISTANCE / max_exact)
                         * (n_buckets - max_exact)).astype(jnp.int32)
    large = jnp.minimum(large, n_buckets - 1)
    bucket = jnp.where(n < max_exact, n, large)
    t = jnp.zeros((2 * blk, blk), F32)
    for k in range(n_buckets):
        t = jnp.where(bucket == k, (rb_ref[h, k] - far) * LOG2_E, t)
    o_ref[...] = jnp.where(rel >= 0, t, -jnp.inf)


def _bias_tables(rel_bias):
    n_buckets, n_heads = rel_bias.shape
    return pl.pallas_call(
        functools.partial(_bias_kernel, n_buckets=n_buckets),
        grid=(n_heads,),
        in_specs=[pl.BlockSpec(memory_space=pltpu.SMEM)],
        out_specs=pl.BlockSpec((None, 2 * MOBA_BLOCK, MOBA_BLOCK), lambda h: (h, 0, 0)),
        out_shape=jax.ShapeDtypeStruct((n_heads, 2 * MOBA_BLOCK, MOBA_BLOCK), F32),
        compiler_params=_params("parallel"),
        name="bias_tables",
    )(rel_bias.T.astype(F32))


def _attn_kernel(q_ref, k_ref, vt_ref, tab_ref, o_ref, kmean_ref, chosen_ref, m_ref, l_ref,
                 acc_ref, *, nb, heads):
    i = pl.program_id(2)
    blk = MOBA_BLOCK
    pair = 2 * blk
    nt = (((1,), (1,)), ((), ()))

    @pl.when(i == 0)
    def _():
        for hh in range(heads):
            for jb in range(nb):
                kb = k_ref[hh, jb * blk:(jb + 1) * blk, :].astype(F32)
                kmean_ref[hh, jb:jb + 1, :] = jnp.mean(kb, axis=0, keepdims=True)

    row = lax.broadcasted_iota(jnp.int32, (nb, blk), 0)
    rowf = row.astype(F32)
    past = row < i
    prev = pl.multiple_of(jnp.maximum(i - 1, 0) * blk, blk)
    own = pl.multiple_of(i * blk, blk)
    qs = [q_ref[hh] for hh in range(heads)]

    def partial_softmax(blocks):
        m = functools.reduce(jnp.maximum, [jnp.max(s, axis=0, keepdims=True) for s in blocks])
        m_safe = jnp.where(m == -jnp.inf, 0.0, m)
        ps = [jnp.exp2(s - m_safe) for s in blocks]
        total = functools.reduce(jnp.add, [jnp.sum(p, axis=0, keepdims=True) for p in ps])
        return m, total, jnp.concatenate([p.astype(BF16) for p in ps], axis=0)


    first_scores = []
    for hh in range(heads):
        q = qs[hh]
        gate = lax.dot_general(kmean_ref[hh], q.astype(F32), nt, preferred_element_type=F32,
                               precision=lax.Precision.HIGHEST)
        g = jnp.where(past, gate, -jnp.inf)
        chosen = jnp.zeros((nb, blk), F32)
        for _ in range(MOBA_TOPK):
            top = jnp.max(g, axis=0, keepdims=True)
            idx = jnp.min(jnp.where(g == top, rowf, float(nb)), axis=0, keepdims=True)
            pick = rowf == idx
            chosen = jnp.where(pick, 1.0, chosen)
            g = jnp.where(pick, -jnp.inf, g)
        chosen = jnp.where(past, chosen, 0.0)
        sel_prev = jnp.sum(jnp.where(row == i - 1, chosen, 0.0), axis=0, keepdims=True)
        chosen_ref[hh] = jnp.where(row >= i - 1, 0.0, chosen)
        kk = jnp.concatenate([k_ref[hh, pl.ds(prev, blk), :], k_ref[hh, pl.ds(own, blk), :]],
                             axis=0)
        s = lax.dot_general(kk, q, nt, preferred_element_type=F32) + tab_ref[hh]
        first_scores.append([jnp.where(sel_prev > 0.0, s[:blk, :], -jnp.inf), s[blk:, :]])
    def far_scores(hh, j0):
        start = pl.multiple_of(j0 * blk, pair)
        s = lax.dot_general(k_ref[hh, pl.ds(start, pair), :], qs[hh], nt,
                            preferred_element_type=F32)
        return [jnp.where(chosen_ref[hh, pl.ds(j0 + n, 1), :] > 0.0,
                          s[n * blk:(n + 1) * blk, :], -jnp.inf) for n in range(2)]

    def quad_groups(t):
        return [(hh, 4 * t + half) for half in (0, 2) for hh in range(heads)]

    first_parts = [partial_softmax(s) for s in first_scores]
    for hh in range(heads):
        m, l, p = first_parts[hh]
        vv = jnp.concatenate([vt_ref[hh, :, pl.ds(prev, blk)], vt_ref[hh, :, pl.ds(own, blk)]],
                             axis=1)
        m_ref[hh] = m
        l_ref[hh] = l
        acc_ref[hh] = jnp.dot(vv, p, preferred_element_type=F32)

    def far_quad(t, carry):
        groups = quad_groups(t)
        parts = [partial_softmax(far_scores(hh, j0)) for hh, j0 in groups]
        outs = [jnp.dot(vt_ref[hh, :, pl.ds(pl.multiple_of(j0 * blk, pair), pair)], p,
                        preferred_element_type=F32)
                for (hh, j0), (_, _, p) in zip(groups, parts)]
        for hh in range(heads):
            mine = [n for n, (gh, _) in enumerate(groups) if gh == hh]
            m_old = m_ref[hh]
            m_new = m_old
            for n in mine:
                m_new = jnp.maximum(m_new, parts[n][0])
            w_old = jnp.exp2(m_old - m_new)
            l = w_old * l_ref[hh]
            acc = w_old * acc_ref[hh]
            for n in mine:
                w = jnp.exp2(parts[n][0] - m_new)
                l = l + w * parts[n][1]
                acc = acc + w * outs[n]
            m_ref[hh] = m_new
            l_ref[hh] = l
            acc_ref[hh] = acc
        return carry

    lax.fori_loop(0, (i + 2) // 4, far_quad, 0)
    for hh in range(heads):
        o_ref[:, hh * HEAD_DIM:(hh + 1) * HEAD_DIM] = (acc_ref[hh] / l_ref[hh]).T.astype(BF16)


def _moba_attention(kvq, tables):
    _, bsz, n_heads, seq, _ = kvq.shape
    nb = seq // MOBA_BLOCK
    assert seq % MOBA_BLOCK == 0 and nb % 4 == 0
    vt = jnp.swapaxes(kvq[1], -1, -2)
    heads = 2 if n_heads % 2 == 0 else 1
    return pl.pallas_call(
        functools.partial(_attn_kernel, nb=nb, heads=heads),
        grid=(bsz, n_heads // heads, nb),
        in_specs=[
            pl.BlockSpec((None, None, heads, MOBA_BLOCK, HEAD_DIM),
                         lambda b, h, i: (2, b, h, i, 0)),
            pl.BlockSpec((None, None, heads, seq, HEAD_DIM), lambda b, h, i: (0, b, h, 0, 0)),
            pl.BlockSpec((None, heads, HEAD_DIM, seq), lambda b, h, i: (b, h, 0, 0)),
            pl.BlockSpec((heads, 2 * MOBA_BLOCK, MOBA_BLOCK), lambda b, h, i: (h, 0, 0)),
        ],
        out_specs=pl.BlockSpec((None, MOBA_BLOCK, heads * HEAD_DIM), lambda b, h, i: (b, i, h)),
        out_shape=jax.ShapeDtypeStruct((bsz, seq, n_heads * HEAD_DIM), BF16),
        scratch_shapes=[
            pltpu.VMEM((heads, nb, HEAD_DIM), F32),
            pltpu.VMEM((heads, nb, MOBA_BLOCK), F32),
            pltpu.VMEM((heads, 1, MOBA_BLOCK), F32),
            pltpu.VMEM((heads, 1, MOBA_BLOCK), F32),
            pltpu.VMEM((heads, HEAD_DIM, MOBA_BLOCK), F32),
        ],
        compiler_params=_params("parallel", "parallel", "arbitrary"),
        name="moba_attention",
    )(kvq, kvq, vt, tables)


def _oproj_kernel(a_ref, w_ref, x_ref, gate_ref, ng_ref, sh_ref, sc_ref, wr_ref,
                  x3_ref, h_ref, lg_ref, xs_ref, *, nj, tn):
    j = pl.program_id(1)
    y = jnp.dot(a_ref[...], w_ref[...], preferred_element_type=F32)
    x3 = x_ref[...] + gate_ref[...] * y
    x3_ref[...] = x3
    xs_ref[j] = x3

    @pl.when(j == nj - 1)
    def _():
        d = nj * tn
        ss = jnp.zeros((x3.shape[0], 1), F32)
        for jj in range(nj):
            xj = xs_ref[jj]
            ss = ss + jnp.sum(xj * xj, axis=-1, keepdims=True)
        inv = lax.rsqrt(ss / d + EPS)
        logits = jnp.zeros(lg_ref.shape, F32)
        for jj in range(nj):
            cols = slice(jj * tn, (jj + 1) * tn)
            hj = (xs_ref[jj] * inv * ng_ref[:, cols]) * (1.0 + sc_ref[:, cols]) + sh_ref[:, cols]
            hi = hj.astype(BF16)
            lo = (hj - hi.astype(F32)).astype(BF16)
            h_ref[:, cols] = hi
            w = wr_ref[cols, :]
            whi = w.astype(BF16)
            wlo = (w - whi.astype(F32)).astype(BF16)
            logits = (logits + jnp.dot(hi, whi, preferred_element_type=F32)
                      + (jnp.dot(hi, wlo, preferred_element_type=F32)
                         + jnp.dot(lo, whi, preferred_element_type=F32)))
        lg_ref[...] = logits


def _oproj_layer(attn, x, gate, norm_g, sh, sc, w_o, router_w):
    bsz, seq, d = x.shape
    n_exp = router_w.shape[1]
    assert n_exp <= LANES
    wr = jnp.pad(router_w, ((0, 0), (0, LANES - n_exp)))
    tm = _tile(seq, 512)
    tn = _tile(d, 512)
    nj = d // tn
    per_b = seq // tm
    tok = bsz * seq
    vec_t = pl.BlockSpec((None, 1, tn), lambda i, j: (i // per_b, 0, j))
    vec = pl.BlockSpec((None, 1, d), lambda i, j: (i // per_b, 0, 0))
    x3, h, logits = pl.pallas_call(
        functools.partial(_oproj_kernel, nj=nj, tn=tn),
        grid=(bsz * per_b, nj),
        in_specs=[
            pl.BlockSpec((tm, d), lambda i, j: (i, 0)),
            pl.BlockSpec((d, tn), lambda i, j: (0, j)),
            pl.BlockSpec((tm, tn), lambda i, j: (i, j)),
            vec_t,
            pl.BlockSpec((1, d), lambda i, j: (0, 0)),
            vec, vec,
            pl.BlockSpec((d, LANES), lambda i, j: (0, 0)),
        ],
        out_specs=[
            pl.BlockSpec((tm, tn), lambda i, j: (i, j)),
            pl.BlockSpec((tm, d), lambda i, j: (i, 0)),
            pl.BlockSpec((tm, LANES), lambda i, j: (i, 0)),
        ],
        out_shape=[
            jax.ShapeDtypeStruct((tok, d), F32),
            jax.ShapeDtypeStruct((tok, d), BF16),
            jax.ShapeDtypeStruct((tok, LANES), F32),
        ],
        scratch_shapes=[pltpu.VMEM((nj, tm, tn), F32)],
        compiler_params=_params("parallel", "arbitrary"),
        name="oproj_prenorm_router",
    )(attn.reshape(tok, d), w_o.astype(BF16), x.reshape(tok, d), gate, norm_g.reshape(1, d),
      sh, sc, wr)
    return x3, h, logits


ROUTE_E0, ROUTE_E1, ROUTE_W0, ROUTE_W1, ROUTE_R0, ROUTE_R1 = range(6)


def _route_kernel(lg_ref, rb_ref, route_ref, cnt_ref, carry_ref, *, n_exp):
    step = pl.program_id(0)
    tr = lg_ref.shape[0]

    @pl.when(step == 0)
    def _():
        carry_ref[...] = jnp.zeros_like(carry_ref)

    lane = lax.broadcasted_iota(jnp.int32, (tr, LANES), 1)
    lanef = lane.astype(F32)
    logits = jnp.where(lane < n_exp, lg_ref[...] + rb_ref[...], -jnp.inf)
    m0 = jnp.max(logits, axis=1, keepdims=True)
    i0 = jnp.min(jnp.where(logits == m0, lanef, float(LANES)), axis=1, keepdims=True)
    oh0 = lanef == i0
    rest = jnp.where(oh0, -jnp.inf, logits)
    m1 = jnp.max(rest, axis=1, keepdims=True)
    i1 = jnp.min(jnp.where(rest == m1, lanef, float(LANES)), axis=1, keepdims=True)
    oh1 = lanef == i1
    e1 = jnp.exp(m1 - m0)
    w0 = 1.0 / (1.0 + e1)
    w1 = e1 / (1.0 + e1)
    oh = jnp.where(oh0 | oh1, 1.0, 0.0)
    r = lax.broadcasted_iota(jnp.int32, (tr, tr), 0)
    c = lax.broadcasted_iota(jnp.int32, (tr, tr), 1)
    earlier = jnp.where(c < r, 1.0, 0.0).astype(BF16)
    base = carry_ref[...] + jnp.dot(earlier, oh.astype(BF16), preferred_element_type=F32)
    r0 = jnp.sum(jnp.where(oh0, base, 0.0), axis=1, keepdims=True)
    r1 = jnp.sum(jnp.where(oh1, base, 0.0), axis=1, keepdims=True)
    total = carry_ref[...] + jnp.sum(oh, axis=0, keepdims=True)
    carry_ref[...] = total
    cnt_ref[...] = jnp.broadcast_to(total, cnt_ref.shape)
    out = jnp.zeros((tr, LANES), F32)
    for slot, val in ((ROUTE_E0, i0), (ROUTE_E1, i1), (ROUTE_W0, w0),
                      (ROUTE_W1, w1), (ROUTE_R0, r0), (ROUTE_R1, r1)):
        out = jnp.where(lane == slot, val, out)
    route_ref[...] = out


def _route(logits, router_b):
    tok = logits.shape[0]
    n_exp = router_b.shape[0]
    tr = _tile(tok, 512)
    rb = jnp.pad(router_b.astype(F32), (0, LANES - n_exp)).reshape(1, LANES)
    return pl.pallas_call(
        functools.partial(_route_kernel, n_exp=n_exp),
        grid=(tok // tr,),
        in_specs=[pl.BlockSpec((tr, LANES), lambda s: (s, 0)),
                  pl.BlockSpec((1, LANES), lambda s: (0, 0))],
        out_specs=[pl.BlockSpec((tr, LANES), lambda s: (s, 0)),
                   pl.BlockSpec((SUBLANES, LANES), lambda s: (0, 0))],
        out_shape=[jax.ShapeDtypeStruct((tok, LANES), F32),
                   jax.ShapeDtypeStruct((SUBLANES, LANES), F32)],
        scratch_shapes=[pltpu.VMEM((1, LANES), F32)],
        compiler_params=_params("arbitrary"),
        name="route_top2",
    )(logits, rb)


def _row_copy_kernel(start_ref, idx_ref, src_ref, *rest, td, gather):
    dst_ref, sem = rest[-2:]

    def copy(r, k):
        pos = start_ref[idx_ref[k, r]] + idx_ref[TOP_K + k, r]
        if gather:
            return pltpu.make_async_copy(src_ref.at[pos], dst_ref.at[k, r], sem)
        return pltpu.make_async_copy(src_ref.at[r], dst_ref.at[pos], sem)

    def issue(r, carry):
        for k in range(TOP_K):
            copy(r, k).start()
        return carry

    def drain(r, carry):
        for k in range(TOP_K):
            copy(r, k).wait()
        return carry

    lax.fori_loop(0, td, issue, 0)
    lax.fori_loop(0, td, drain, 0)


def _row_copy(src3, idx, row_start, *, gather, rows_out):
    n_steps, _, td = idx.shape
    slab = src3.shape[1:]
    any_spec = pl.BlockSpec(memory_space=pl.ANY)
    idx_spec = pl.BlockSpec((None, 2 * TOP_K, td), lambda s, start: (s, 0, 0),
                            memory_space=pltpu.SMEM)
    if gather:
        out_shape = jax.ShapeDtypeStruct((TOP_K, n_steps * td) + slab, src3.dtype)
        operands = [row_start, idx, src3]
        in_specs = [idx_spec, any_spec]
        out_spec = pl.BlockSpec((TOP_K, td) + slab, lambda s, start: (0, s, 0, 0))
        aliases = {}
    else:
        out_shape = jax.ShapeDtypeStruct((rows_out,) + slab, src3.dtype)
        operands = [row_start, idx, src3, jnp.zeros(out_shape.shape, out_shape.dtype)]
        in_specs = [idx_spec, pl.BlockSpec((td,) + slab, lambda s, start: (s, 0, 0)), any_spec]
        out_spec = any_spec
        aliases = {3: 0}
    return pl.pallas_call(
        functools.partial(_row_copy_kernel, td=td, gather=gather),
        grid_spec=pltpu.PrefetchScalarGridSpec(
            num_scalar_prefetch=1,
            grid=(n_steps,),
            in_specs=in_specs,
            out_specs=out_spec,
            scratch_shapes=[pltpu.SemaphoreType.DMA(())],
        ),
        out_shape=out_shape,
        input_output_aliases=aliases,
        compiler_params=pltpu.CompilerParams(dimension_semantics=("arbitrary",),
                                             vmem_limit_bytes=VMEM_LIMIT,
                                             has_side_effects=True),
        name="moe_gather" if gather else "moe_dispatch",
    )(*operands)


def _moe_gateup_kernel(te_ref, na_ref, h_ref, w1_ref, w3_ref, o_ref):
    active = pl.program_id(0) < na_ref[0]

    @pl.when(active)
    def _():
        h = h_ref[...]
        a = jnp.dot(h, w1_ref[...], preferred_element_type=F32)
        b = jnp.dot(h, w3_ref[...], preferred_element_type=F32)
        o_ref[...] = _silu_mul(a, b).astype(BF16)

    @pl.when(jnp.logical_not(active))
    def _():
        o_ref[...] = jnp.zeros_like(o_ref)


def _moe_down_kernel(te_ref, na_ref, g_ref, w_ref, o_ref):
    active = pl.program_id(0) < na_ref[0]

    @pl.when(active)
    def _():
        o_ref[...] = jnp.dot(g_ref[...], w_ref[...], preferred_element_type=F32).astype(BF16)

    @pl.when(jnp.logical_not(active))
    def _():
        o_ref[...] = jnp.zeros_like(o_ref)


def _moe_ffn(hs, tile_expert, n_active, w1, w3, w2, *, tm):
    rows, d = hs.shape
    tn = _tile(w1.shape[-1], 512)
    w1, w3, w2 = _swiglu_weights(w1, w3, w2, tn)
    ff = w1.shape[-1]
    tn2 = _tile(d, 1024)
    n_tiles = rows // tm

    def w_map(i, j, te, na):
        return (te[i], 0, jnp.where(i < na[0], j, 0))

    w_up = pl.BlockSpec((None, d, tn), w_map)
    hidden = pl.pallas_call(
        _moe_gateup_kernel,
        grid_spec=pltpu.PrefetchScalarGridSpec(
            num_scalar_prefetch=2,
            grid=(n_tiles, ff // tn),
            in_specs=[pl.BlockSpec((tm, d), lambda i, j, te, na: (i, 0)), w_up, w_up],
            out_specs=pl.BlockSpec((tm, tn), lambda i, j, te, na: (i, j)),
        ),
        out_shape=jax.ShapeDtypeStruct((rows, ff), BF16),
        compiler_params=_params("parallel", "arbitrary"),
        name="moe_gate_up",
    )(tile_expert, n_active, hs, w1, w3)
    return pl.pallas_call(
        _moe_down_kernel,
        grid_spec=pltpu.PrefetchScalarGridSpec(
            num_scalar_prefetch=2,
            grid=(n_tiles, d // tn2),
            in_specs=[
                pl.BlockSpec((tm, ff), lambda i, j, te, na: (i, 0)),
                pl.BlockSpec((None, ff, tn2), w_map),
            ],
            out_specs=pl.BlockSpec((tm, tn2), lambda i, j, te, na: (i, j)),
        ),
        out_shape=jax.ShapeDtypeStruct((rows, d), BF16),
        compiler_params=_params("parallel", "arbitrary"),
        name="moe_down",
    )(tile_expert, n_active, hidden, w2)


def _combine_kernel(x_ref, y_ref, route_ref, gate_ref, o_ref):
    route = route_ref[...]
    lane = lax.broadcasted_iota(jnp.int32, route.shape, 1)
    w0 = jnp.sum(jnp.where(lane == ROUTE_W0, route, 0.0), axis=1, keepdims=True)
    w1 = jnp.sum(jnp.where(lane == ROUTE_W1, route, 0.0), axis=1, keepdims=True)
    y = w0 * y_ref[0].astype(F32) + w1 * y_ref[1].astype(F32)
    o_ref[...] = x_ref[...] + gate_ref[...] * y


def _combine(x3, y2, route, gate, *, seq):
    tok, d = x3.shape
    tm = _tile(seq, 256)
    per_b = seq // tm
    return pl.pallas_call(
        _combine_kernel,
        grid=(tok // tm,),
        in_specs=[
            pl.BlockSpec((tm, d), lambda i: (i, 0)),
            pl.BlockSpec((TOP_K, tm, d), lambda i: (0, i, 0)),
            pl.BlockSpec((tm, LANES), lambda i: (i, 0)),
            pl.BlockSpec((None, 1, d), lambda i: (i // per_b, 0, 0)),
        ],
        out_specs=pl.BlockSpec((tm, d), lambda i: (i, 0)),
        out_shape=jax.ShapeDtypeStruct((tok, d), F32),
        compiler_params=_params("parallel"),
        name="moe_combine",
    )(x3, y2, route, gate)


def _moe_layer(x3, h, logits, gate, router_b, w1, w3, w2, *, seq):
    tok, d = x3.shape
    n_exp = router_b.shape[0]
    slabs = d // LANES
    tm = 512
    route, counts = _route(logits, router_b)

    cnt = counts[0, :n_exp].astype(jnp.int32)
    tiles_per = (cnt + tm - 1) // tm
    tile_end = jnp.cumsum(tiles_per)
    row_start = (tile_end - tiles_per) * tm
    n_tiles = (TOP_K * tok) // tm + n_exp
    tile_expert = jnp.minimum(
        jnp.searchsorted(tile_end, jnp.arange(n_tiles, dtype=jnp.int32), side="right"),
        n_exp - 1).astype(jnp.int32)
    n_active = tile_end[-1:].astype(jnp.int32)

    td = _tile(tok, 256)
    idx = jnp.concatenate([route[:, ROUTE_E0:ROUTE_E1 + 1], route[:, ROUTE_R0:ROUTE_R1 + 1]],
                          axis=1).astype(jnp.int32)
    idx = idx.reshape(tok // td, td, 2 * TOP_K).transpose(0, 2, 1)

    rows = n_tiles * tm
    hs3 = _row_copy(h.reshape(tok, slabs, LANES), idx, row_start, gather=False, rows_out=rows)
    ys = _moe_ffn(hs3.reshape(rows, d), tile_expert, n_active, w1, w3, w2, tm=tm)
    y4 = _row_copy(ys.reshape(rows, slabs, LANES), idx, row_start, gather=True, rows_out=None)
    return _combine(x3, y4.reshape(TOP_K, tok, d), route, gate, seq=seq)


def kernel(x, c, ada_w, ada_b, norm_mix_g, norm_ffn_g, pool_w, pool_b, pool_scale, kv_ada_w,
           kv_ada_b, kv_norm_g, w_k, w_v, k_norm_g, w_q, q_norm_g, w_o, rel_bias, ffn_w1, ffn_w3,
           ffn_w2, router_w, router_b, moe_w1, moe_w3, moe_w2):
    bsz, seq, d = x.shape
    assert ada_w.shape[0] == 2 and pool_w.shape[0] == 1 and w_q.shape[0] == 1

    mod = _adaln(c, ada_w, ada_b)
    kv_mod = _adaln(c, kv_ada_w[None], kv_ada_b[None])[0]

    def vecs(m, n):
        return [m[:, None, k * d:(k + 1) * d] for k in range(n)]

    sh_m0, sc_m0, g_m0, sh_f0, sc_f0, g_f0 = vecs(mod[0], 6)
    sh_m1, sc_m1, g_m1, sh_f1, sc_f1, g_f1 = vecs(mod[1], 6)
    sh_kv, sc_kv = vecs(kv_mod, 2)

    x1 = _pool_layer(x, sh_m0, sc_m0, g_m0, norm_mix_g[0], pool_w[0], pool_b[0], pool_scale[0])
    x2 = _ffn_layer(x1, sh_f0, sc_f0, g_f0, norm_ffn_g[0], ffn_w1[0], ffn_w3[0], ffn_w2[0])

    kvq = _kvq_proj(x2, sh_kv, sc_kv, sh_m1, sc_m1, kv_norm_g, norm_mix_g[1], k_norm_g,
                    q_norm_g[0], w_k, w_v, w_q[0])
    attn = _moba_attention(kvq, _bias_tables(rel_bias))
    x3, h, logits = _oproj_layer(attn, x2, g_m1, norm_ffn_g[1], sh_f1, sc_f1, w_o[0], router_w[0])

    out = _moe_layer(x3, h, logits, g_f1, router_b[0], moe_w1[0], moe_w3[0], moe_w2[0], seq=seq)
    return out.reshape(bsz, seq, d)
```

```python
import functools
import math

import jax
import jax.numpy as jnp
from jax import lax
from jax.experimental import pallas as pl
from jax.experimental.pallas import tpu as pltpu

F32 = jnp.float32
BF16 = jnp.bfloat16

EPS = 1e-6
POOL_WINDOWS = (2, 4, 8, 16)
HEAD_DIM = 128
MOBA_BLOCK = 256
MOBA_TOPK = 3
MAX_DISTANCE = 128
TOP_K = 2
LOG2_E = math.log2(math.e)
Q_SCALE = HEAD_DIM ** -0.5 * LOG2_E

LANES = 128
SUBLANES = 8
POOL_HALO = 16
VMEM_LIMIT = 56 * 1024 * 1024


def _params(*sem):
    return pltpu.CompilerParams(dimension_semantics=sem, vmem_limit_bytes=VMEM_LIMIT)


def _tile(n, pref):
    t = min(n, pref)
    while n % t:
        t //= 2
    return t


def _rms_mod(x, g, shift, scale):
    ms = jnp.mean(x * x, axis=-1, keepdims=True)
    return (x * lax.rsqrt(ms + EPS) * g) * (1.0 + scale) + shift


def _adaln_kernel(c_ref, w_ref, b_ref, o_ref):
    c = c_ref[...]
    s = c * (1.0 / (1.0 + jnp.exp(-c)))
    w = w_ref[...]
    s_hi = s.astype(BF16)
    s_lo = (s - s_hi.astype(F32)).astype(BF16)
    w_hi = w.astype(BF16)
    w_lo = (w - w_hi.astype(F32)).astype(BF16)
    both = jnp.dot(jnp.concatenate([s_hi, s_lo], axis=0), w_hi, preferred_element_type=F32)
    rows = s.shape[0]
    o_ref[...] = (both[:rows] + both[rows:]
                  + jnp.dot(s_hi, w_lo, preferred_element_type=F32)) + b_ref[...]


def _adaln(c, w, b):
    n_layers, d, n = w.shape
    bsz = c.shape[0]
    assert bsz <= SUBLANES
    cp = jnp.pad(c, ((0, SUBLANES - bsz), (0, 0)))
    tn = _tile(n, 512)
    out = pl.pallas_call(
        _adaln_kernel,
        grid=(n_layers, n // tn),
        in_specs=[
            pl.BlockSpec((SUBLANES, d), lambda l, j: (0, 0)),
            pl.BlockSpec((None, d, tn), lambda l, j: (l, 0, j)),
            pl.BlockSpec((None, 1, tn), lambda l, j: (l, 0, j)),
        ],
        out_specs=pl.BlockSpec((None, SUBLANES, tn), lambda l, j: (l, 0, j)),
        out_shape=jax.ShapeDtypeStruct((n_layers, SUBLANES, n), F32),
        compiler_params=_params("parallel", "parallel"),
        name="adaln",
    )(cp, w, b.reshape(n_layers, 1, n))
    return out[:, :bsz, :]


def _pool_kernel(x_ref, halo_ref, sh_ref, sc_ref, gate_ref, ng_ref, w_ref, pb_ref, ls_ref,
                 o_ref, ext_ref, *, tm, group):
    i = pl.program_id(1)
    ng, sh, sc = ng_ref[...], sh_ref[...], sc_ref[...]
    x = x_ref[...]
    ext_ref[0:POOL_HALO, :] = jnp.where(i > 0, _rms_mod(halo_ref[...], ng, sh, sc), 0.0)
    ext_ref[POOL_HALO:, :] = _rms_mod(x, ng, sh, sc)
    pos = lax.broadcasted_iota(jnp.int32, (tm, 1), 0) + i * tm
    for g, win in enumerate(POOL_WINDOWS):
        cols = slice(g * group, (g + 1) * group)
        e = ext_ref[:, cols]
        s, span = e, 1
        while span < win:
            s = s + pltpu.roll(s, span, 0)
            span *= 2
        count = jnp.minimum(pos + 1, win).astype(F32)
        p = s[POOL_HALO:, :] / count - e[POOL_HALO:, :]
        y = jnp.dot(p.astype(BF16), w_ref[g], preferred_element_type=F32) + pb_ref[g]
        o_ref[:, cols] = x[:, cols] + gate_ref[:, cols] * (y * ls_ref[:, cols])


def _pool_layer(x, sh, sc, gate, norm_g, pool_w, pool_b, pool_scale):
    bsz, seq, d = x.shape
    n_groups = len(POOL_WINDOWS)
    group = d // n_groups
    tm = _tile(seq, 256)
    assert tm % POOL_HALO == 0 and max(POOL_WINDOWS) <= POOL_HALO
    vec = pl.BlockSpec((None, 1, d), lambda b, i: (b, 0, 0))
    const = pl.BlockSpec((1, d), lambda b, i: (0, 0))
    return pl.pallas_call(
        functools.partial(_pool_kernel, tm=tm, group=group),
        grid=(bsz, seq // tm),
        in_specs=[
            pl.BlockSpec((None, tm, d), lambda b, i: (b, i, 0)),
            pl.BlockSpec((None, POOL_HALO, d),
                         lambda b, i: (b, jnp.maximum(i * (tm // POOL_HALO) - 1, 0), 0)),
            vec, vec, vec, const,
            pl.BlockSpec((n_groups, group, group), lambda b, i: (0, 0, 0)),
            pl.BlockSpec((n_groups, 1, group), lambda b, i: (0, 0, 0)),
            const,
        ],
        out_specs=pl.BlockSpec((None, tm, d), lambda b, i: (b, i, 0)),
        out_shape=jax.ShapeDtypeStruct((bsz, seq, d), F32),
        scratch_shapes=[pltpu.VMEM((POOL_HALO + tm, d), F32)],
        compiler_params=_params("parallel", "parallel"),
        name="pool_mixer",
    )(x, x, sh, sc, gate, norm_g.reshape(1, d), pool_w.astype(BF16),
      pool_b.reshape(n_groups, 1, group), pool_scale.reshape(1, d))


def _silu_mul(a, b):
    return (a * (1.0 / (1.0 + jnp.exp(-a)))) * b


def _swiglu_weights(w1, w3, w2, tn):
    pad = (-w1.shape[-1]) % tn
    cols = [(0, 0)] * (w1.ndim - 1) + [(0, pad)]
    rows = [(0, 0)] * (w2.ndim - 2) + [(0, pad), (0, 0)]
    w1, w3, w2 = w1.astype(BF16), w3.astype(BF16), w2.astype(BF16)
    if pad:
        w1, w3, w2 = jnp.pad(w1, cols), jnp.pad(w3, cols), jnp.pad(w2, rows)
    return w1, w3, w2


def _gateup_kernel(x_ref, sh_ref, sc_ref, ng_ref, w1_ref, w3_ref, o_ref, h_ref, *, rows):
    @pl.when(pl.program_id(1) == 0)
    def _():
        for r in range(0, h_ref.shape[0], rows):
            h_ref[r:r + rows, :] = _rms_mod(x_ref[r:r + rows, :], ng_ref[...], sh_ref[...],
                                            sc_ref[...]).astype(BF16)

    h = h_ref[...]
    a = jnp.dot(h, w1_ref[...], preferred_element_type=F32)
    b = jnp.dot(h, w3_ref[...], preferred_element_type=F32)
    o_ref[...] = _silu_mul(a, b).astype(BF16)


def _down_kernel(g_ref, w_ref, x_ref, gate_ref, o_ref):
    o_ref[...] = x_ref[...] + gate_ref[...] * jnp.dot(g_ref[...], w_ref[...],
                                                      preferred_element_type=F32)


def _ffn_layer(x, sh, sc, gate, norm_g, w1, w3, w2):
    bsz, seq, d = x.shape
    tok = bsz * seq
    tn = 256
    w1, w3, w2 = _swiglu_weights(w1, w3, w2, tn)
    ff = w1.shape[-1]
    tm = _tile(seq, 1024)
    per_b = seq // tm
    vec = pl.BlockSpec((None, 1, d), lambda i, j: (i // per_b, 0, 0))
    w_up = pl.BlockSpec((d, tn), lambda i, j: (0, j))
    x2 = x.reshape(tok, d)
    hidden = pl.pallas_call(
        functools.partial(_gateup_kernel, rows=_tile(tm, 256)),
        grid=(tok // tm, ff // tn),
        in_specs=[
            pl.BlockSpec((tm, d), lambda i, j: (i, 0), pipeline_mode=pl.Buffered(1)),
            vec, vec,
            pl.BlockSpec((1, d), lambda i, j: (0, 0)),
            w_up, w_up,
        ],
        out_specs=pl.BlockSpec((tm, tn), lambda i, j: (i, j)),
        out_shape=jax.ShapeDtypeStruct((tok, ff), BF16),
        scratch_shapes=[pltpu.VMEM((tm, d), BF16)],
        compiler_params=_params("parallel", "arbitrary"),
        name="ffn_gate_up",
    )(x2, sh, sc, norm_g.reshape(1, d), w1, w3)

    tn2 = _tile(d, 512)
    tm2 = _tile(seq, 512)
    per_b2 = seq // tm2
    out = pl.pallas_call(
        _down_kernel,
        grid=(tok // tm2, d // tn2),
        in_specs=[
            pl.BlockSpec((tm2, ff), lambda i, j: (i, 0)),
            pl.BlockSpec((ff, tn2), lambda i, j: (0, j)),
            pl.BlockSpec((tm2, tn2), lambda i, j: (i, j)),
            pl.BlockSpec((None, 1, tn2), lambda i, j: (i // per_b2, 0, j)),
        ],
        out_specs=pl.BlockSpec((tm2, tn2), lambda i, j: (i, j)),
        out_shape=jax.ShapeDtypeStruct((tok, d), F32),
        compiler_params=_params("parallel", "arbitrary"),
        name="ffn_down",
    )(hidden, w2, x2, gate)
    return out.reshape(bsz, seq, d)


def _kvq_kernel(x_ref, shk_ref, sck_ref, shq_ref, scq_ref, ngk_ref, ngq_ref, gk_ref, gq_ref,
                w_ref, o_ref, h_ref, *, nk):
    j = pl.program_id(1)

    @pl.when(j == 0)
    def _():
        rows = _tile(x_ref.shape[0], 128)
        for r in range(0, x_ref.shape[0], rows):
            x = x_ref[r:r + rows, :]
            xn = x * lax.rsqrt(jnp.mean(x * x, axis=-1, keepdims=True) + EPS)
            h_ref[0, r:r + rows, :] = ((xn * ngk_ref[...]) * (1.0 + sck_ref[...])
                                       + shk_ref[...]).astype(BF16)
            h_ref[1, r:r + rows, :] = ((xn * ngq_ref[...]) * (1.0 + scq_ref[...])
                                       + shq_ref[...]).astype(BF16)

    kind = j // nk
    r = jnp.dot(h_ref[kind // 2], w_ref[...], preferred_element_type=F32)
    gvec = jnp.where(kind == 2, gq_ref[...] * Q_SCALE, gk_ref[...])
    for hh in range(o_ref.shape[0]):
        rh = r[:, hh * HEAD_DIM:(hh + 1) * HEAD_DIM]
        ms = jnp.mean(rh * rh, axis=-1, keepdims=True)
        normed = rh * lax.rsqrt(ms + EPS) * gvec
        o_ref[hh] = jnp.where(kind == 1, rh, normed).astype(BF16)


def _kvq_proj(x, sh_kv, sc_kv, sh_q, sc_q, ng_kv, ng_q, k_norm_g, q_norm_g, w_k, w_v, w_q):
    bsz, seq, d = x.shape
    n_heads = d // HEAD_DIM
    w = jnp.concatenate([w_k.astype(BF16), w_v.astype(BF16), w_q.astype(BF16)], axis=1)
    tm = _tile(seq, 512)
    tn = _tile(d, 1024)
    nk = d // tn
    per_b = seq // tm
    hpt = tn // HEAD_DIM
    vec = pl.BlockSpec((None, 1, d), lambda i, j: (i // per_b, 0, 0))
    const = pl.BlockSpec((1, d), lambda i, j: (0, 0))
    hvec = pl.BlockSpec((1, HEAD_DIM), lambda i, j: (0, 0))
    return pl.pallas_call(
        functools.partial(_kvq_kernel, nk=nk),
        grid=(bsz * per_b, 3 * nk),
        in_specs=[
            pl.BlockSpec((tm, d), lambda i, j: (i, 0), pipeline_mode=pl.Buffered(1)),
            vec, vec, vec, vec, const, const, hvec, hvec,
            pl.BlockSpec((d, tn), lambda i, j: (0, j)),
        ],
        out_specs=pl.BlockSpec((None, None, hpt, tm, HEAD_DIM),
                               lambda i, j: (j // nk, i // per_b, j % nk, i % per_b, 0)),
        out_shape=jax.ShapeDtypeStruct((3, bsz, n_heads, seq, HEAD_DIM), BF16),
        scratch_shapes=[pltpu.VMEM((2, tm, d), BF16)],
        compiler_params=_params("parallel", "arbitrary"),
        name="kvq_proj",
    )(x.reshape(bsz * seq, d), sh_kv, sc_kv, sh_q, sc_q, ng_kv.reshape(1, d), ng_q.reshape(1, d),
      k_norm_g.reshape(1, HEAD_DIM), q_norm_g.reshape(1, HEAD_DIM), w)


def _bias_kernel(rb_ref, o_ref, *, n_buckets):
    h = pl.program_id(0)
    blk = o_ref.shape[-1]
    key = lax.broadcasted_iota(jnp.int32, (2 * blk, blk), 0)
    qry = lax.broadcasted_iota(jnp.int32, (2 * blk, blk), 1)
    rel = qry - key + blk
    max_exact = n_buckets // 2
    far = rb_ref[h, n_buckets - 1]
    n = jnp.maximum(rel, 0)
    nf = jnp.maximum(n, 1).astype(F32)
    large = max_exact + (jnp.log(nf / max_exact) / math.log(MAX_DISTANCE / max_exact)
                         * (n_buckets - max_exact)).astype(jnp.int32)
    large = jnp.minimum(large, n_buckets - 1)
    bucket = jnp.where(n < max_exact, n, large)
    t = jnp.zeros((2 * blk, blk), F32)
    for k in range(n_buckets):
        t = jnp.where(bucket == k, (rb_ref[h, k] - far) * LOG2_E, t)
    o_ref[...] = jnp.where(rel >= 0, t, -jnp.inf)


def _bias_tables(rel_bias):
    n_buckets, n_heads = rel_bias.shape
    return pl.pallas_call(
        functools.partial(_bias_kernel, n_buckets=n_buckets),
        grid=(n_heads,),
        in_specs=[pl.BlockSpec(memory_space=pltpu.SMEM)],
        out_specs=pl.BlockSpec((None, 2 * MOBA_BLOCK, MOBA_BLOCK), lambda h: (h, 0, 0)),
        out_shape=jax.ShapeDtypeStruct((n_heads, 2 * MOBA_BLOCK, MOBA_BLOCK), F32),
        compiler_params=_params("parallel"),
        name="bias_tables",
    )(rel_bias.T.astype(F32))


def _attn_kernel(q_ref, k_ref, vt_ref, tab_ref, *rest, nb, heads, n_side):
    side_in = rest[:n_side]
    o_ref = rest[n_side]
    side_out = rest[n_side + 1:2 * n_side + 1]
    kmean_ref, chosen_ref, m_ref, l_ref, acc_ref = rest[-5:]
    for src, dst in zip(side_in, side_out):
        dst[...] = src[...].astype(BF16)
    if n_side:
        zero_ref = rest[2 * n_side + 1]
        zero_ref[...] = jnp.zeros_like(zero_ref)

    i = pl.program_id(2)
    blk = MOBA_BLOCK
    pair = 2 * blk
    nt = (((1,), (1,)), ((), ()))

    @pl.when(i == 0)
    def _():
        for hh in range(heads):
            for jb in range(nb):
                kb = k_ref[hh, jb * blk:(jb + 1) * blk, :].astype(F32)
                kmean_ref[hh, jb:jb + 1, :] = jnp.mean(kb, axis=0, keepdims=True)

    row = lax.broadcasted_iota(jnp.int32, (nb, blk), 0)
    rowf = row.astype(F32)
    past = row < i
    prev = pl.multiple_of(jnp.maximum(i - 1, 0) * blk, blk)
    own = pl.multiple_of(i * blk, blk)
    qs = [q_ref[hh] for hh in range(heads)]

    def partial_softmax(blocks):
        m = functools.reduce(jnp.maximum, [jnp.max(s, axis=0, keepdims=True) for s in blocks])
        m_safe = jnp.where(m == -jnp.inf, 0.0, m)
        ps = [jnp.exp2(s - m_safe) for s in blocks]
        total = functools.reduce(jnp.add, [jnp.sum(p, axis=0, keepdims=True) for p in ps])
        return m, total, jnp.concatenate([p.astype(BF16) for p in ps], axis=0)


    first_scores = []
    for hh in range(heads):
        q = qs[hh]
        gate = lax.dot_general(kmean_ref[hh], q.astype(F32), nt, preferred_element_type=F32,
                               precision=lax.Precision.HIGHEST)
        g = jnp.where(past, gate, -jnp.inf)
        chosen = jnp.zeros((nb, blk), F32)
        for _ in range(MOBA_TOPK):
            top = jnp.max(g, axis=0, keepdims=True)
            idx = jnp.min(jnp.where(g == top, rowf, float(nb)), axis=0, keepdims=True)
            pick = rowf == idx
            chosen = jnp.where(pick, 1.0, chosen)
            g = jnp.where(pick, -jnp.inf, g)
        chosen = jnp.where(past, chosen, 0.0)
        sel_prev = jnp.sum(jnp.where(row == i - 1, chosen, 0.0), axis=0, keepdims=True)
        chosen_ref[hh] = jnp.where(row >= i - 1, 0.0, chosen)
        kk = jnp.concatenate([k_ref[hh, pl.ds(prev, blk), :], k_ref[hh, pl.ds(own, blk), :]],
                             axis=0)
        s = lax.dot_general(kk, q, nt, preferred_element_type=F32) + tab_ref[hh]
        first_scores.append([jnp.where(sel_prev > 0.0, s[:blk, :], -jnp.inf), s[blk:, :]])
    def far_scores(hh, j0):
        start = pl.multiple_of(j0 * blk, pair)
        s = lax.dot_general(k_ref[hh, pl.ds(start, pair), :], qs[hh], nt,
                            preferred_element_type=F32)
        return [jnp.where(chosen_ref[hh, pl.ds(j0 + n, 1), :] > 0.0,
                          s[n * blk:(n + 1) * blk, :], -jnp.inf) for n in range(2)]

    def quad_groups(t):
        return [(hh, 4 * t + half) for half in (0, 2) for hh in range(heads)]

    first_parts = [partial_softmax(s) for s in first_scores]
    for hh in range(heads):
        m, l, p = first_parts[hh]
        vv = jnp.concatenate([vt_ref[hh, :, pl.ds(prev, blk)], vt_ref[hh, :, pl.ds(own, blk)]],
                             axis=1)
        m_ref[hh] = m
        l_ref[hh] = l
        acc_ref[hh] = jnp.dot(vv, p, preferred_element_type=F32)

    def far_quad(t, carry):
        groups = quad_groups(t)
        parts = [partial_softmax(far_scores(hh, j0)) for hh, j0 in groups]
        outs = [jnp.dot(vt_ref[hh, :, pl.ds(pl.multiple_of(j0 * blk, pair), pair)], p,
                        preferred_element_type=F32)
                for (hh, j0), (_, _, p) in zip(groups, parts)]
        for hh in range(heads):
            mine = [n for n, (gh, _) in enumerate(groups) if gh == hh]
            m_old = m_ref[hh]
            m_new = m_old
            for n in mine:
                m_new = jnp.maximum(m_new, parts[n][0])
            w_old = jnp.exp2(m_old - m_new)
            l = w_old * l_ref[hh]
            acc = w_old * acc_ref[hh]
            for n in mine:
                w = jnp.exp2(parts[n][0] - m_new)
                l = l + w * parts[n][1]
                acc = acc + w * outs[n]
            m_ref[hh] = m_new
            l_ref[hh] = l
            acc_ref[hh] = acc
        return carry

    lax.fori_loop(0, (i + 2) // 4, far_quad, 0)
    for hh in range(heads):
        o_ref[:, hh * HEAD_DIM:(hh + 1) * HEAD_DIM] = (acc_ref[hh] / l_ref[hh]).T.astype(BF16)


def _row_chunks(shape, n_steps, sublanes):
    rows = math.prod(shape[:-1])
    if rows % n_steps or (rows // n_steps) % sublanes:
        return None
    return (n_steps, rows // n_steps, shape[-1])


def _moba_attention(kvq, tables, later_weights, zero_shape):
    _, bsz, n_heads, seq, _ = kvq.shape
    nb = seq // MOBA_BLOCK
    assert seq % MOBA_BLOCK == 0 and nb % 4 == 0
    vt = jnp.swapaxes(kvq[1], -1, -2)
    heads = 2 if n_heads % 2 == 0 else 1
    n_pairs = n_heads // heads
    n_steps = bsz * n_pairs * nb
    packed_rows = 2 * SUBLANES
    views = [_row_chunks(w.shape, n_steps, packed_rows) for w in later_weights]
    zero_rows = zero_shape[0] // n_steps
    if any(v is None for v in views) or zero_shape[0] % n_steps:
        views, side = [], []
    else:
        side = [w.reshape(v) for w, v in zip(later_weights, views)]

    def chunk(shape):
        return pl.BlockSpec((None,) + tuple(shape[1:]),
                            lambda b, h, i: ((b * n_pairs + h) * nb + i, 0, 0))

    zero_view = (n_steps, zero_rows) + tuple(zero_shape[1:])
    out = pl.pallas_call(
        functools.partial(_attn_kernel, nb=nb, heads=heads, n_side=len(side)),
        grid=(bsz, n_pairs, nb),
        in_specs=[
            pl.BlockSpec((None, None, heads, MOBA_BLOCK, HEAD_DIM),
                         lambda b, h, i: (2, b, h, i, 0)),
            pl.BlockSpec((None, None, heads, seq, HEAD_DIM), lambda b, h, i: (0, b, h, 0, 0)),
            pl.BlockSpec((None, heads, HEAD_DIM, seq), lambda b, h, i: (b, h, 0, 0)),
            pl.BlockSpec((heads, 2 * MOBA_BLOCK, MOBA_BLOCK), lambda b, h, i: (h, 0, 0)),
        ] + [chunk(v) for v in views],
        out_specs=[pl.BlockSpec((None, MOBA_BLOCK, heads * HEAD_DIM), lambda b, h, i: (b, i, h))]
        + [chunk(v) for v in views]
        + ([pl.BlockSpec((None,) + zero_view[1:],
                         lambda b, h, i: ((b * n_pairs + h) * nb + i, 0, 0, 0))] if side else []),
        out_shape=[jax.ShapeDtypeStruct((bsz, seq, n_heads * HEAD_DIM), BF16)]
        + [jax.ShapeDtypeStruct(v, BF16) for v in views]
        + ([jax.ShapeDtypeStruct(zero_view, BF16)] if side else []),
        scratch_shapes=[
            pltpu.VMEM((heads, nb, HEAD_DIM), F32),
            pltpu.VMEM((heads, nb, MOBA_BLOCK), F32),
            pltpu.VMEM((heads, 1, MOBA_BLOCK), F32),
            pltpu.VMEM((heads, 1, MOBA_BLOCK), F32),
            pltpu.VMEM((heads, HEAD_DIM, MOBA_BLOCK), F32),
        ],
        compiler_params=_params("parallel", "parallel", "arbitrary"),
        name="moba_attention",
    )(kvq, kvq, vt, tables, *side)
    if not side:
        return (out[0], [w.astype(BF16) for w in later_weights], jnp.zeros(zero_shape, BF16))
    weights = [o.reshape(w.shape) for o, w in zip(out[1:-1], later_weights)]
    return out[0], weights, out[-1].reshape(zero_shape)


def _oproj_kernel(a_ref, w_ref, x_ref, gate_ref, ng_ref, sh_ref, sc_ref, wr_ref,
                  x3_ref, h_ref, lg_ref, xs_ref, *, nj, tn):
    j = pl.program_id(1)
    y = jnp.dot(a_ref[...], w_ref[...], preferred_element_type=F32)
    x3 = x_ref[...] + gate_ref[...] * y
    x3_ref[...] = x3
    xs_ref[j] = x3

    @pl.when(j == nj - 1)
    def _():
        d = nj * tn
        ss = jnp.zeros((x3.shape[0], 1), F32)
        for jj in range(nj):
            xj = xs_ref[jj]
            ss = ss + jnp.sum(xj * xj, axis=-1, keepdims=True)
        inv = lax.rsqrt(ss / d + EPS)
        logits = jnp.zeros(lg_ref.shape, F32)
        for jj in range(nj):
            cols = slice(jj * tn, (jj + 1) * tn)
            hj = (xs_ref[jj] * inv * ng_ref[:, cols]) * (1.0 + sc_ref[:, cols]) + sh_ref[:, cols]
            hi = hj.astype(BF16)
            lo = (hj - hi.astype(F32)).astype(BF16)
            h_ref[:, cols] = hi
            w = wr_ref[cols, :]
            whi = w.astype(BF16)
            wlo = (w - whi.astype(F32)).astype(BF16)
            logits = (logits + jnp.dot(hi, whi, preferred_element_type=F32)
                      + (jnp.dot(hi, wlo, preferred_element_type=F32)
                         + jnp.dot(lo, whi, preferred_element_type=F32)))
        lg_ref[...] = logits


def _oproj_layer(attn, x, gate, norm_g, sh, sc, w_o, router_w):
    bsz, seq, d = x.shape
    n_exp = router_w.shape[1]
    assert n_exp <= LANES
    wr = jnp.pad(router_w, ((0, 0), (0, LANES - n_exp)))
    tm = _tile(seq, 512)
    tn = _tile(d, 512)
    nj = d // tn
    per_b = seq // tm
    tok = bsz * seq
    vec_t = pl.BlockSpec((None, 1, tn), lambda i, j: (i // per_b, 0, j))
    vec = pl.BlockSpec((None, 1, d), lambda i, j: (i // per_b, 0, 0))
    x3, h, logits = pl.pallas_call(
        functools.partial(_oproj_kernel, nj=nj, tn=tn),
        grid=(bsz * per_b, nj),
        in_specs=[
            pl.BlockSpec((tm, d), lambda i, j: (i, 0)),
            pl.BlockSpec((d, tn), lambda i, j: (0, j)),
            pl.BlockSpec((tm, tn), lambda i, j: (i, j)),
            vec_t,
            pl.BlockSpec((1, d), lambda i, j: (0, 0)),
            vec, vec,
            pl.BlockSpec((d, LANES), lambda i, j: (0, 0)),
        ],
        out_specs=[
            pl.BlockSpec((tm, tn), lambda i, j: (i, j)),
            pl.BlockSpec((tm, d), lambda i, j: (i, 0)),
            pl.BlockSpec((tm, LANES), lambda i, j: (i, 0)),
        ],
        out_shape=[
            jax.ShapeDtypeStruct((tok, d), F32),
            jax.ShapeDtypeStruct((tok, d), BF16),
            jax.ShapeDtypeStruct((tok, LANES), F32),
        ],
        scratch_shapes=[pltpu.VMEM((nj, tm, tn), F32)],
        compiler_params=_params("parallel", "arbitrary"),
        name="oproj_prenorm_router",
    )(attn.reshape(tok, d), w_o.astype(BF16), x.reshape(tok, d), gate, norm_g.reshape(1, d),
      sh, sc, wr)
    return x3, h, logits


ROUTE_E0, ROUTE_E1, ROUTE_W0, ROUTE_W1, ROUTE_R0, ROUTE_R1 = range(6)


def _route_kernel(lg_ref, rb_ref, route_ref, cnt_ref, carry_ref, *, n_exp):
    step = pl.program_id(0)
    tr = lg_ref.shape[0]

    @pl.when(step == 0)
    def _():
        carry_ref[...] = jnp.zeros_like(carry_ref)

    lane = lax.broadcasted_iota(jnp.int32, (tr, LANES), 1)
    lanef = lane.astype(F32)
    logits = jnp.where(lane < n_exp, lg_ref[...] + rb_ref[...], -jnp.inf)
    m0 = jnp.max(logits, axis=1, keepdims=True)
    i0 = jnp.min(jnp.where(logits == m0, lanef, float(LANES)), axis=1, keepdims=True)
    oh0 = lanef == i0
    rest = jnp.where(oh0, -jnp.inf, logits)
    m1 = jnp.max(rest, axis=1, keepdims=True)
    i1 = jnp.min(jnp.where(rest == m1, lanef, float(LANES)), axis=1, keepdims=True)
    oh1 = lanef == i1
    e1 = jnp.exp(m1 - m0)
    w0 = 1.0 / (1.0 + e1)
    w1 = e1 / (1.0 + e1)
    oh = jnp.where(oh0 | oh1, 1.0, 0.0)
    r = lax.broadcasted_iota(jnp.int32, (tr, tr), 0)
    c = lax.broadcasted_iota(jnp.int32, (tr, tr), 1)
    earlier = jnp.where(c < r, 1.0, 0.0).astype(BF16)
    base = carry_ref[...] + jnp.dot(earlier, oh.astype(BF16), preferred_element_type=F32)
    r0 = jnp.sum(jnp.where(oh0, base, 0.0), axis=1, keepdims=True)
    r1 = jnp.sum(jnp.where(oh1, base, 0.0), axis=1, keepdims=True)
    total = carry_ref[...] + jnp.sum(oh, axis=0, keepdims=True)
    carry_ref[...] = total
    cnt_ref[...] = jnp.broadcast_to(total, cnt_ref.shape)
    out = jnp.zeros((tr, LANES), F32)
    for slot, val in ((ROUTE_E0, i0), (ROUTE_E1, i1), (ROUTE_W0, w0),
                      (ROUTE_W1, w1), (ROUTE_R0, r0), (ROUTE_R1, r1)):
        out = jnp.where(lane == slot, val, out)
    route_ref[...] = out


def _route(logits, router_b):
    tok = logits.shape[0]
    n_exp = router_b.shape[0]
    tr = _tile(tok, 512)
    rb = jnp.pad(router_b.astype(F32), (0, LANES - n_exp)).reshape(1, LANES)
    return pl.pallas_call(
        functools.partial(_route_kernel, n_exp=n_exp),
        grid=(tok // tr,),
        in_specs=[pl.BlockSpec((tr, LANES), lambda s: (s, 0)),
                  pl.BlockSpec((1, LANES), lambda s: (0, 0))],
        out_specs=[pl.BlockSpec((tr, LANES), lambda s: (s, 0)),
                   pl.BlockSpec((SUBLANES, LANES), lambda s: (0, 0))],
        out_shape=[jax.ShapeDtypeStruct((tok, LANES), F32),
                   jax.ShapeDtypeStruct((SUBLANES, LANES), F32)],
        scratch_shapes=[pltpu.VMEM((1, LANES), F32)],
        compiler_params=_params("arbitrary"),
        name="route_top2",
    )(logits, rb)


def _row_copy_kernel(start_ref, idx_ref, src_ref, *rest, td, gather):
    dst_ref, sem = rest[-2:]

    def copy(r, k):
        pos = start_ref[idx_ref[k, r]] + idx_ref[TOP_K + k, r]
        if gather:
            return pltpu.make_async_copy(src_ref.at[pos], dst_ref.at[k, r], sem)
        return pltpu.make_async_copy(src_ref.at[r], dst_ref.at[pos], sem)

    def issue(r, carry):
        for k in range(TOP_K):
            copy(r, k).start()
        return carry

    def drain(r, carry):
        for k in range(TOP_K):
            copy(r, k).wait()
        return carry

    lax.fori_loop(0, td, issue, 0)
    lax.fori_loop(0, td, drain, 0)


def _row_copy(src3, idx, row_start, scatter_into=None):
    n_steps, _, td = idx.shape
    slab = src3.shape[1:]
    gather = scatter_into is None
    any_spec = pl.BlockSpec(memory_space=pl.ANY)
    idx_spec = pl.BlockSpec((None, 2 * TOP_K, td), lambda s, start: (s, 0, 0),
                            memory_space=pltpu.SMEM)
    if gather:
        out_shape = jax.ShapeDtypeStruct((TOP_K, n_steps * td) + slab, src3.dtype)
        operands = [row_start, idx, src3]
        in_specs = [idx_spec, any_spec]
        out_spec = pl.BlockSpec((TOP_K, td) + slab, lambda s, start: (0, s, 0, 0))
        aliases = {}
    else:
        out_shape = jax.ShapeDtypeStruct(scatter_into.shape, src3.dtype)
        operands = [row_start, idx, src3, scatter_into]
        in_specs = [idx_spec, pl.BlockSpec((td,) + slab, lambda s, start: (s, 0, 0)), any_spec]
        out_spec = any_spec
        aliases = {3: 0}
    return pl.pallas_call(
        functools.partial(_row_copy_kernel, td=td, gather=gather),
        grid_spec=pltpu.PrefetchScalarGridSpec(
            num_scalar_prefetch=1,
            grid=(n_steps,),
            in_specs=in_specs,
            out_specs=out_spec,
            scratch_shapes=[pltpu.SemaphoreType.DMA(())],
        ),
        out_shape=out_shape,
        input_output_aliases=aliases,
        compiler_params=pltpu.CompilerParams(dimension_semantics=("arbitrary",),
                                             vmem_limit_bytes=VMEM_LIMIT,
                                             has_side_effects=True),
        name="moe_gather" if gather else "moe_dispatch",
    )(*operands)


def _moe_gateup_kernel(te_ref, na_ref, h_ref, w1_ref, w3_ref, o_ref):
    active = pl.program_id(0) < na_ref[0]

    @pl.when(active)
    def _():
        h = h_ref[...]
        a = jnp.dot(h, w1_ref[...], preferred_element_type=F32)
        b = jnp.dot(h, w3_ref[...], preferred_element_type=F32)
        o_ref[...] = _silu_mul(a, b).astype(BF16)

    @pl.when(jnp.logical_not(active))
    def _():
        o_ref[...] = jnp.zeros_like(o_ref)


def _moe_down_kernel(te_ref, na_ref, g_ref, w_ref, o_ref):
    active = pl.program_id(0) < na_ref[0]

    @pl.when(active)
    def _():
        o_ref[...] = jnp.dot(g_ref[...], w_ref[...], preferred_element_type=F32).astype(BF16)

    @pl.when(jnp.logical_not(active))
    def _():
        o_ref[...] = jnp.zeros_like(o_ref)


def _moe_ffn(hs, tile_expert, n_active, w1, w3, w2, *, tm):
    rows, d = hs.shape
    tn = _tile(w1.shape[-1], 512)
    w1, w3, w2 = _swiglu_weights(w1, w3, w2, tn)
    ff = w1.shape[-1]
    tn2 = _tile(d, 1024)
    n_tiles = rows // tm

    def w_map(i, j, te, na):
        return (te[i], 0, jnp.where(i < na[0], j, 0))

    w_up = pl.BlockSpec((None, d, tn), w_map)
    hidden = pl.pallas_call(
        _moe_gateup_kernel,
        grid_spec=pltpu.PrefetchScalarGridSpec(
            num_scalar_prefetch=2,
            grid=(n_tiles, ff // tn),
            in_specs=[pl.BlockSpec((tm, d), lambda i, j, te, na: (i, 0)), w_up, w_up],
            out_specs=pl.BlockSpec((tm, tn), lambda i, j, te, na: (i, j)),
        ),
        out_shape=jax.ShapeDtypeStruct((rows, ff), BF16),
        compiler_params=_params("parallel", "arbitrary"),
        name="moe_gate_up",
    )(tile_expert, n_active, hs, w1, w3)
    return pl.pallas_call(
        _moe_down_kernel,
        grid_spec=pltpu.PrefetchScalarGridSpec(
            num_scalar_prefetch=2,
            grid=(n_tiles, d // tn2),
            in_specs=[
                pl.BlockSpec((tm, ff), lambda i, j, te, na: (i, 0)),
                pl.BlockSpec((None, ff, tn2), w_map),
            ],
            out_specs=pl.BlockSpec((tm, tn2), lambda i, j, te, na: (i, j)),
        ),
        out_shape=jax.ShapeDtypeStruct((rows, d), BF16),
        compiler_params=_params("parallel", "arbitrary"),
        name="moe_down",
    )(tile_expert, n_active, hidden, w2)


def _combine_kernel(x_ref, y_ref, route_ref, gate_ref, o_ref):
    route = route_ref[...]
    lane = lax.broadcasted_iota(jnp.int32, route.shape, 1)
    w0 = jnp.sum(jnp.where(lane == ROUTE_W0, route, 0.0), axis=1, keepdims=True)
    w1 = jnp.sum(jnp.where(lane == ROUTE_W1, route, 0.0), axis=1, keepdims=True)
    y = w0 * y_ref[0].astype(F32) + w1 * y_ref[1].astype(F32)
    o_ref[...] = x_ref[...] + gate_ref[...] * y


def _combine(x3, y2, route, gate, *, seq):
    tok, d = x3.shape
    tm = _tile(seq, 256)
    per_b = seq // tm
    return pl.pallas_call(
        _combine_kernel,
        grid=(tok // tm,),
        in_specs=[
            pl.BlockSpec((tm, d), lambda i: (i, 0)),
            pl.BlockSpec((TOP_K, tm, d), lambda i: (0, i, 0)),
            pl.BlockSpec((tm, LANES), lambda i: (i, 0)),
            pl.BlockSpec((None, 1, d), lambda i: (i // per_b, 0, 0)),
        ],
        out_specs=pl.BlockSpec((tm, d), lambda i: (i, 0)),
        out_shape=jax.ShapeDtypeStruct((tok, d), F32),
        compiler_params=_params("parallel"),
        name="moe_combine",
    )(x3, y2, route, gate)


MOE_TILE = 512


def _moe_rows(tok, n_exp):
    return ((TOP_K * tok) // MOE_TILE + n_exp) * MOE_TILE


def _moe_layer(x3, h, logits, gate, router_b, w1, w3, w2, sorted_zeros, *, seq):
    tok, d = x3.shape
    n_exp = router_b.shape[0]
    slabs = d // LANES
    tm = MOE_TILE
    route, counts = _route(logits, router_b)

    cnt = counts[0, :n_exp].astype(jnp.int32)
    tiles_per = (cnt + tm - 1) // tm
    tile_end = jnp.cumsum(tiles_per)
    row_start = (tile_end - tiles_per) * tm
    n_tiles = _moe_rows(tok, n_exp) // tm
    tile_expert = jnp.minimum(
        jnp.searchsorted(tile_end, jnp.arange(n_tiles, dtype=jnp.int32), side="right"),
        n_exp - 1).astype(jnp.int32)
    n_active = tile_end[-1:].astype(jnp.int32)

    td = _tile(tok, 256)
    idx = jnp.concatenate([route[:, ROUTE_E0:ROUTE_E1 + 1], route[:, ROUTE_R0:ROUTE_R1 + 1]],
                          axis=1).astype(jnp.int32)
    idx = idx.reshape(tok // td, td, 2 * TOP_K).transpose(0, 2, 1)

    rows = n_tiles * tm
    hs3 = _row_copy(h.reshape(tok, slabs, LANES), idx, row_start, scatter_into=sorted_zeros)
    ys = _moe_ffn(hs3.reshape(rows, d), tile_expert, n_active, w1, w3, w2, tm=tm)
    y4 = _row_copy(ys.reshape(rows, slabs, LANES), idx, row_start)
    return _combine(x3, y4.reshape(TOP_K, tok, d), route, gate, seq=seq)


def kernel(x, c, ada_w, ada_b, norm_mix_g, norm_ffn_g, pool_w, pool_b, pool_scale, kv_ada_w,
           kv_ada_b, kv_norm_g, w_k, w_v, k_norm_g, w_q, q_norm_g, w_o, rel_bias, ffn_w1, ffn_w3,
           ffn_w2, router_w, router_b, moe_w1, moe_w3, moe_w2):
    bsz, seq, d = x.shape
    assert ada_w.shape[0] == 2 and pool_w.shape[0] == 1 and w_q.shape[0] == 1

    mod = _adaln(c, ada_w, ada_b)
    kv_mod = _adaln(c, kv_ada_w[None], kv_ada_b[None])[0]

    def vecs(m, n):
        return [m[:, None, k * d:(k + 1) * d] for k in range(n)]

    sh_m0, sc_m0, g_m0, sh_f0, sc_f0, g_f0 = vecs(mod[0], 6)
    sh_m1, sc_m1, g_m1, sh_f1, sc_f1, g_f1 = vecs(mod[1], 6)
    sh_kv, sc_kv = vecs(kv_mod, 2)

    x1 = _pool_layer(x, sh_m0, sc_m0, g_m0, norm_mix_g[0], pool_w[0], pool_b[0], pool_scale[0])
    x2 = _ffn_layer(x1, sh_f0, sc_f0, g_f0, norm_ffn_g[0], ffn_w1[0], ffn_w3[0], ffn_w2[0])

    kvq = _kvq_proj(x2, sh_kv, sc_kv, sh_m1, sc_m1, kv_norm_g, norm_mix_g[1], k_norm_g,
                    q_norm_g[0], w_k, w_v, w_q[0])
    sorted_shape = (_moe_rows(bsz * seq, router_b.shape[-1]), d // LANES, LANES)
    attn, (e_w1, e_w3, e_w2), sorted_zeros = _moba_attention(
        kvq, _bias_tables(rel_bias), (moe_w1[0], moe_w3[0], moe_w2[0]), sorted_shape)
    x3, h, logits = _oproj_layer(attn, x2, g_m1, norm_ffn_g[1], sh_f1, sc_f1, w_o[0], router_w[0])

    out = _moe_layer(x3, h, logits, g_f1, router_b[0], e_w1, e_w3, e_w2, sorted_zeros, seq=seq)
    return out.reshape(bsz, seq, d)
```

```python
import functools
import math

import jax
import jax.numpy as jnp
from jax import lax
from jax.experimental import pallas as pl
from jax.experimental.pallas import tpu as pltpu

F32 = jnp.float32
BF16 = jnp.bfloat16

EPS = 1e-6
POOL_WINDOWS = (2, 4, 8, 16)
HEAD_DIM = 128
MOBA_BLOCK = 256
MOBA_TOPK = 3
MAX_DISTANCE = 128
TOP_K = 2
LOG2_E = math.log2(math.e)
Q_SCALE = HEAD_DIM ** -0.5 * LOG2_E

LANES = 128
SUBLANES = 8
POOL_HALO = 16
VMEM_LIMIT = 56 * 1024 * 1024


def _params(*sem):
    return pltpu.CompilerParams(dimension_semantics=sem, vmem_limit_bytes=VMEM_LIMIT)


def _tile(n, pref):
    t = min(n, pref)
    while n % t:
        t //= 2
    return t


def _rms_mod(x, g, shift, scale):
    ms = jnp.mean(x * x, axis=-1, keepdims=True)
    return (x * lax.rsqrt(ms + EPS) * g) * (1.0 + scale) + shift


def _adaln_kernel(c_ref, w_ref, b_ref, o_ref):
    c = c_ref[...]
    s = c * (1.0 / (1.0 + jnp.exp(-c)))
    w = w_ref[...]
    s_hi = s.astype(BF16)
    s_lo = (s - s_hi.astype(F32)).astype(BF16)
    w_hi = w.astype(BF16)
    w_lo = (w - w_hi.astype(F32)).astype(BF16)
    both = jnp.dot(jnp.concatenate([s_hi, s_lo], axis=0), w_hi, preferred_element_type=F32)
    rows = s.shape[0]
    o_ref[...] = (both[:rows] + both[rows:]
                  + jnp.dot(s_hi, w_lo, preferred_element_type=F32)) + b_ref[...]


def _adaln(c, w, b):
    n_layers, d, n = w.shape
    bsz = c.shape[0]
    assert bsz <= SUBLANES
    cp = jnp.pad(c, ((0, SUBLANES - bsz), (0, 0)))
    tn = _tile(n, 512)
    out = pl.pallas_call(
        _adaln_kernel,
        grid=(n_layers, n // tn),
        in_specs=[
            pl.BlockSpec((SUBLANES, d), lambda l, j: (0, 0)),
            pl.BlockSpec((None, d, tn), lambda l, j: (l, 0, j)),
            pl.BlockSpec((None, 1, tn), lambda l, j: (l, 0, j)),
        ],
        out_specs=pl.BlockSpec((None, SUBLANES, tn), lambda l, j: (l, 0, j)),
        out_shape=jax.ShapeDtypeStruct((n_layers, SUBLANES, n), F32),
        compiler_params=_params("parallel", "parallel"),
        name="adaln",
    )(cp, w, b.reshape(n_layers, 1, n))
    return out[:, :bsz, :]


def _pool_kernel(x_ref, halo_ref, sh_ref, sc_ref, gate_ref, ng_ref, w_ref, pb_ref, ls_ref,
                 o_ref, ext_ref, *, tm, group):
    i = pl.program_id(1)
    ng, sh, sc = ng_ref[...], sh_ref[...], sc_ref[...]
    x = x_ref[...]
    ext_ref[0:POOL_HALO, :] = jnp.where(i > 0, _rms_mod(halo_ref[...], ng, sh, sc), 0.0)
    ext_ref[POOL_HALO:, :] = _rms_mod(x, ng, sh, sc)
    pos = lax.broadcasted_iota(jnp.int32, (tm, 1), 0) + i * tm
    for g, win in enumerate(POOL_WINDOWS):
        cols = slice(g * group, (g + 1) * group)
        e = ext_ref[:, cols]
        s, span = e, 1
        while span < win:
            s = s + pltpu.roll(s, span, 0)
            span *= 2
        count = jnp.minimum(pos + 1, win).astype(F32)
        p = s[POOL_HALO:, :] / count - e[POOL_HALO:, :]
        y = jnp.dot(p.astype(BF16), w_ref[g], preferred_element_type=F32) + pb_ref[g]
        o_ref[:, cols] = x[:, cols] + gate_ref[:, cols] * (y * ls_ref[:, cols])


def _pool_layer(x, sh, sc, gate, norm_g, pool_w, pool_b, pool_scale):
    bsz, seq, d = x.shape
    n_groups = len(POOL_WINDOWS)
    group = d // n_groups
    tm = _tile(seq, 256)
    assert tm % POOL_HALO == 0 and max(POOL_WINDOWS) <= POOL_HALO
    vec = pl.BlockSpec((None, 1, d), lambda b, i: (b, 0, 0))
    const = pl.BlockSpec((1, d), lambda b, i: (0, 0))
    return pl.pallas_call(
        functools.partial(_pool_kernel, tm=tm, group=group),
        grid=(bsz, seq // tm),
        in_specs=[
            pl.BlockSpec((None, tm, d), lambda b, i: (b, i, 0)),
            pl.BlockSpec((None, POOL_HALO, d),
                         lambda b, i: (b, jnp.maximum(i * (tm // POOL_HALO) - 1, 0), 0)),
            vec, vec, vec, const,
            pl.BlockSpec((n_groups, group, group), lambda b, i: (0, 0, 0)),
            pl.BlockSpec((n_groups, 1, group), lambda b, i: (0, 0, 0)),
            const,
        ],
        out_specs=pl.BlockSpec((None, tm, d), lambda b, i: (b, i, 0)),
        out_shape=jax.ShapeDtypeStruct((bsz, seq, d), F32),
        scratch_shapes=[pltpu.VMEM((POOL_HALO + tm, d), F32)],
        compiler_params=_params("parallel", "parallel"),
        name="pool_mixer",
    )(x, x, sh, sc, gate, norm_g.reshape(1, d), pool_w.astype(BF16),
      pool_b.reshape(n_groups, 1, group), pool_scale.reshape(1, d))


def _silu_mul(a, b):
    return (a * (1.0 / (1.0 + jnp.exp(-a)))) * b


def _swiglu_weights(w1, w3, w2, tn):
    pad = (-w1.shape[-1]) % tn
    cols = [(0, 0)] * (w1.ndim - 1) + [(0, pad)]
    rows = [(0, 0)] * (w2.ndim - 2) + [(0, pad), (0, 0)]
    w1, w3, w2 = w1.astype(BF16), w3.astype(BF16), w2.astype(BF16)
    if pad:
        w1, w3, w2 = jnp.pad(w1, cols), jnp.pad(w3, cols), jnp.pad(w2, rows)
    return w1, w3, w2


def _gateup_kernel(x_ref, sh_ref, sc_ref, ng_ref, w1_ref, w3_ref, o_ref, h_ref, *, rows):
    @pl.when(pl.program_id(1) == 0)
    def _():
        for r in range(0, h_ref.shape[0], rows):
            h_ref[r:r + rows, :] = _rms_mod(x_ref[r:r + rows, :], ng_ref[...], sh_ref[...],
                                            sc_ref[...]).astype(BF16)

    h = h_ref[...]
    a = jnp.dot(h, w1_ref[...], preferred_element_type=F32)
    b = jnp.dot(h, w3_ref[...], preferred_element_type=F32)
    o_ref[...] = _silu_mul(a, b).astype(BF16)


def _down_kernel(g_ref, w_ref, x_ref, gate_ref, o_ref):
    o_ref[...] = x_ref[...] + gate_ref[...] * jnp.dot(g_ref[...], w_ref[...],
                                                      preferred_element_type=F32)


def _ffn_layer(x, sh, sc, gate, norm_g, w1, w3, w2):
    bsz, seq, d = x.shape
    tok = bsz * seq
    tn = 256
    w1, w3, w2 = _swiglu_weights(w1, w3, w2, tn)
    ff = w1.shape[-1]
    tm = _tile(seq, 1024)
    per_b = seq // tm
    vec = pl.BlockSpec((None, 1, d), lambda i, j: (i // per_b, 0, 0))
    w_up = pl.BlockSpec((d, tn), lambda i, j: (0, j))
    x2 = x.reshape(tok, d)
    hidden = pl.pallas_call(
        functools.partial(_gateup_kernel, rows=_tile(tm, 256)),
        grid=(tok // tm, ff // tn),
        in_specs=[
            pl.BlockSpec((tm, d), lambda i, j: (i, 0), pipeline_mode=pl.Buffered(1)),
            vec, vec,
            pl.BlockSpec((1, d), lambda i, j: (0, 0)),
            w_up, w_up,
        ],
        out_specs=pl.BlockSpec((tm, tn), lambda i, j: (i, j)),
        out_shape=jax.ShapeDtypeStruct((tok, ff), BF16),
        scratch_shapes=[pltpu.VMEM((tm, d), BF16)],
        compiler_params=_params("parallel", "arbitrary"),
        name="ffn_gate_up",
    )(x2, sh, sc, norm_g.reshape(1, d), w1, w3)

    tn2 = _tile(d, 512)
    tm2 = _tile(seq, 512)
    per_b2 = seq // tm2
    out = pl.pallas_call(
        _down_kernel,
        grid=(tok // tm2, d // tn2),
        in_specs=[
            pl.BlockSpec((tm2, ff), lambda i, j: (i, 0)),
            pl.BlockSpec((ff, tn2), lambda i, j: (0, j)),
            pl.BlockSpec((tm2, tn2), lambda i, j: (i, j)),
            pl.BlockSpec((None, 1, tn2), lambda i, j: (i // per_b2, 0, j)),
        ],
        out_specs=pl.BlockSpec((tm2, tn2), lambda i, j: (i, j)),
        out_shape=jax.ShapeDtypeStruct((tok, d), F32),
        compiler_params=_params("parallel", "arbitrary"),
        name="ffn_down",
    )(hidden, w2, x2, gate)
    return out.reshape(bsz, seq, d)


def _kvq_kernel(x_ref, shk_ref, sck_ref, shq_ref, scq_ref, ngk_ref, ngq_ref, gk_ref, gq_ref,
                w_ref, o_ref, h_ref, *, nk):
    j = pl.program_id(1)

    @pl.when(j == 0)
    def _():
        rows = _tile(x_ref.shape[0], 128)
        for r in range(0, x_ref.shape[0], rows):
            x = x_ref[r:r + rows, :]
            xn = x * lax.rsqrt(jnp.mean(x * x, axis=-1, keepdims=True) + EPS)
            h_ref[0, r:r + rows, :] = ((xn * ngk_ref[...]) * (1.0 + sck_ref[...])
                                       + shk_ref[...]).astype(BF16)
            h_ref[1, r:r + rows, :] = ((xn * ngq_ref[...]) * (1.0 + scq_ref[...])
                                       + shq_ref[...]).astype(BF16)

    kind = j // nk
    r = jnp.dot(h_ref[kind // 2], w_ref[...], preferred_element_type=F32)
    gvec = jnp.where(kind == 2, gq_ref[...] * Q_SCALE, gk_ref[...])
    for hh in range(o_ref.shape[0]):
        rh = r[:, hh * HEAD_DIM:(hh + 1) * HEAD_DIM]
        ms = jnp.mean(rh * rh, axis=-1, keepdims=True)
        normed = rh * lax.rsqrt(ms + EPS) * gvec
        o_ref[hh] = jnp.where(kind == 1, rh, normed).astype(BF16)


def _kvq_proj(x, sh_kv, sc_kv, sh_q, sc_q, ng_kv, ng_q, k_norm_g, q_norm_g, w_k, w_v, w_q):
    bsz, seq, d = x.shape
    n_heads = d // HEAD_DIM
    w = jnp.concatenate([w_k.astype(BF16), w_v.astype(BF16), w_q.astype(BF16)], axis=1)
    tm = _tile(seq, 512)
    tn = _tile(d, 1024)
    nk = d // tn
    per_b = seq // tm
    hpt = tn // HEAD_DIM
    vec = pl.BlockSpec((None, 1, d), lambda i, j: (i // per_b, 0, 0))
    const = pl.BlockSpec((1, d), lambda i, j: (0, 0))
    hvec = pl.BlockSpec((1, HEAD_DIM), lambda i, j: (0, 0))
    return pl.pallas_call(
        functools.partial(_kvq_kernel, nk=nk),
        grid=(bsz * per_b, 3 * nk),
        in_specs=[
            pl.BlockSpec((tm, d), lambda i, j: (i, 0), pipeline_mode=pl.Buffered(1)),
            vec, vec, vec, vec, const, const, hvec, hvec,
            pl.BlockSpec((d, tn), lambda i, j: (0, j)),
        ],
        out_specs=pl.BlockSpec((None, None, hpt, tm, HEAD_DIM),
                               lambda i, j: (j // nk, i // per_b, j % nk, i % per_b, 0)),
        out_shape=jax.ShapeDtypeStruct((3, bsz, n_heads, seq, HEAD_DIM), BF16),
        scratch_shapes=[pltpu.VMEM((2, tm, d), BF16)],
        compiler_params=_params("parallel", "arbitrary"),
        name="kvq_proj",
    )(x.reshape(bsz * seq, d), sh_kv, sc_kv, sh_q, sc_q, ng_kv.reshape(1, d), ng_q.reshape(1, d),
      k_norm_g.reshape(1, HEAD_DIM), q_norm_g.reshape(1, HEAD_DIM), w)


def _bias_kernel(rb_ref, o_ref, *, n_buckets):
    h = pl.program_id(0)
    blk = o_ref.shape[-1]
    key = lax.broadcasted_iota(jnp.int32, (2 * blk, blk), 0)
    qry = lax.broadcasted_iota(jnp.int32, (2 * blk, blk), 1)
    rel = qry - key + blk
    max_exact = n_buckets // 2
    far = rb_ref[h, n_buckets - 1]
    n = jnp.maximum(rel, 0)
    nf = jnp.maximum(n, 1).astype(F32)
    large = max_exact + (jnp.log(nf / max_exact) / math.log(MAX_DISTANCE / max_exact)
                         * (n_buckets - max_exact)).astype(jnp.int32)
    large = jnp.minimum(large, n_buckets - 1)
    bucket = jnp.where(n < max_exact, n, large)
    t = jnp.zeros((2 * blk, blk), F32)
    for k in range(n_buckets):
        t = jnp.where(bucket == k, (rb_ref[h, k] - far) * LOG2_E, t)
    o_ref[...] = jnp.where(rel >= 0, t, -jnp.inf)


def _bias_tables(rel_bias):
    n_buckets, n_heads = rel_bias.shape
    return pl.pallas_call(
        functools.partial(_bias_kernel, n_buckets=n_buckets),
        grid=(n_heads,),
        in_specs=[pl.BlockSpec(memory_space=pltpu.SMEM)],
        out_specs=pl.BlockSpec((None, 2 * MOBA_BLOCK, MOBA_BLOCK), lambda h: (h, 0, 0)),
        out_shape=jax.ShapeDtypeStruct((n_heads, 2 * MOBA_BLOCK, MOBA_BLOCK), F32),
        compiler_params=_params("parallel"),
        name="bias_tables",
    )(rel_bias.T.astype(F32))


def _attn_kernel(q_ref, k_ref, vt_ref, tab_ref, *rest, nb, heads, n_side):
    side_in = rest[:n_side]
    o_ref = rest[n_side]
    side_out = rest[n_side + 1:2 * n_side + 1]
    kmean_ref, chosen_ref, m_ref, l_ref, acc_ref = rest[-5:]
    for src, dst in zip(side_in, side_out):
        dst[...] = src[...].astype(BF16)
    if n_side:
        zero_ref = rest[2 * n_side + 1]
        zero_ref[...] = jnp.zeros_like(zero_ref)

    i = pl.program_id(2)
    blk = MOBA_BLOCK
    pair = 2 * blk
    nt = (((1,), (1,)), ((), ()))

    @pl.when(i == 0)
    def _():
        for hh in range(heads):
            for jb in range(nb):
                kb = k_ref[hh, jb * blk:(jb + 1) * blk, :].astype(F32)
                kmean_ref[hh, jb:jb + 1, :] = jnp.mean(kb, axis=0, keepdims=True)

    row = lax.broadcasted_iota(jnp.int32, (nb, blk), 0)
    rowf = row.astype(F32)
    past = row < i
    prev = pl.multiple_of(jnp.maximum(i - 1, 0) * blk, blk)
    own = pl.multiple_of(i * blk, blk)
    qs = [q_ref[hh] for hh in range(heads)]

    def partial_softmax(blocks):
        m = functools.reduce(jnp.maximum, [jnp.max(s, axis=0, keepdims=True) for s in blocks])
        m_safe = jnp.where(m == -jnp.inf, 0.0, m)
        ps = [jnp.exp2(s - m_safe) for s in blocks]
        total = functools.reduce(jnp.add, [jnp.sum(p, axis=0, keepdims=True) for p in ps])
        return m, total, jnp.concatenate([p.astype(BF16) for p in ps], axis=0)


    first_scores = []
    for hh in range(heads):
        q = qs[hh]
        gate = lax.dot_general(kmean_ref[hh], q.astype(F32), nt, preferred_element_type=F32,
                               precision=lax.Precision.HIGHEST)
        g = jnp.where(past, gate, -jnp.inf)
        chosen = jnp.zeros((nb, blk), F32)
        for _ in range(MOBA_TOPK):
            top = jnp.max(g, axis=0, keepdims=True)
            idx = jnp.min(jnp.where(g == top, rowf, float(nb)), axis=0, keepdims=True)
            pick = rowf == idx
            chosen = jnp.where(pick, 1.0, chosen)
            g = jnp.where(pick, -jnp.inf, g)
        chosen = jnp.where(past, chosen, 0.0)
        sel_prev = jnp.sum(jnp.where(row == i - 1, chosen, 0.0), axis=0, keepdims=True)
        chosen_ref[hh] = jnp.where(row >= i - 1, 0.0, chosen)
        kk = jnp.concatenate([k_ref[hh, pl.ds(prev, blk), :], k_ref[hh, pl.ds(own, blk), :]],
                             axis=0)
        s = lax.dot_general(kk, q, nt, preferred_element_type=F32) + tab_ref[hh]
        first_scores.append([jnp.where(sel_prev > 0.0, s[:blk, :], -jnp.inf), s[blk:, :]])
    def far_scores(hh, j0):
        start = pl.multiple_of(j0 * blk, pair)
        s = lax.dot_general(k_ref[hh, pl.ds(start, pair), :], qs[hh], nt,
                            preferred_element_type=F32)
        return [jnp.where(chosen_ref[hh, pl.ds(j0 + n, 1), :] > 0.0,
                          s[n * blk:(n + 1) * blk, :], -jnp.inf) for n in range(2)]

    def quad_groups(t):
        return [(hh, 4 * t + half) for half in (0, 2) for hh in range(heads)]

    first_parts = [partial_softmax(s) for s in first_scores]
    for hh in range(heads):
        m, l, p = first_parts[hh]
        vv = jnp.concatenate([vt_ref[hh, :, pl.ds(prev, blk)], vt_ref[hh, :, pl.ds(own, blk)]],
                             axis=1)
        m_ref[hh] = m
        l_ref[hh] = l
        acc_ref[hh] = jnp.dot(vv, p, preferred_element_type=F32)

    def far_quad(t, carry):
        groups = quad_groups(t)
        parts = [partial_softmax(far_scores(hh, j0)) for hh, j0 in groups]
        outs = [jnp.dot(vt_ref[hh, :, pl.ds(pl.multiple_of(j0 * blk, pair), pair)], p,
                        preferred_element_type=F32)
                for (hh, j0), (_, _, p) in zip(groups, parts)]
        for hh in range(heads):
            mine = [n for n, (gh, _) in enumerate(groups) if gh == hh]
            m_old = m_ref[hh]
            m_new = m_old
            for n in mine:
                m_new = jnp.maximum(m_new, parts[n][0])
            w_old = jnp.exp2(m_old - m_new)
            l = w_old * l_ref[hh]
            acc = w_old * acc_ref[hh]
            for n in mine:
                w = jnp.exp2(parts[n][0] - m_new)
                l = l + w * parts[n][1]
                acc = acc + w * outs[n]
            m_ref[hh] = m_new
            l_ref[hh] = l
            acc_ref[hh] = acc
        return carry

    lax.fori_loop(0, (i + 2) // 4, far_quad, 0)
    for hh in range(heads):
        o_ref[:, hh * HEAD_DIM:(hh + 1) * HEAD_DIM] = (acc_ref[hh] / l_ref[hh]).T.astype(BF16)


def _row_chunks(shape, n_steps, sublanes):
    rows = math.prod(shape[:-1])
    if rows % n_steps or (rows // n_steps) % sublanes:
        return None
    return (n_steps, rows // n_steps, shape[-1])


def _moba_attention(kvq, tables, later_weights, zero_shape):
    _, bsz, n_heads, seq, _ = kvq.shape
    nb = seq // MOBA_BLOCK
    assert seq % MOBA_BLOCK == 0 and nb % 4 == 0
    vt = jnp.swapaxes(kvq[1], -1, -2)
    heads = next(n for n in (4, 2, 1) if n_heads % n == 0)
    n_pairs = n_heads // heads
    n_steps = bsz * n_pairs * nb
    packed_rows = 2 * SUBLANES
    views = [_row_chunks(w.shape, n_steps, packed_rows) for w in later_weights]
    zero_rows = zero_shape[0] // n_steps
    if any(v is None for v in views) or zero_shape[0] % n_steps:
        views, side = [], []
    else:
        side = [w.reshape(v) for w, v in zip(later_weights, views)]

    def chunk(shape):
        return pl.BlockSpec((None,) + tuple(shape[1:]),
                            lambda b, h, i: ((b * n_pairs + h) * nb + i, 0, 0))

    zero_view = (n_steps, zero_rows) + tuple(zero_shape[1:])
    out = pl.pallas_call(
        functools.partial(_attn_kernel, nb=nb, heads=heads, n_side=len(side)),
        grid=(bsz, n_pairs, nb),
        in_specs=[
            pl.BlockSpec((None, None, heads, MOBA_BLOCK, HEAD_DIM),
                         lambda b, h, i: (2, b, h, i, 0)),
            pl.BlockSpec((None, None, heads, seq, HEAD_DIM), lambda b, h, i: (0, b, h, 0, 0)),
            pl.BlockSpec((None, heads, HEAD_DIM, seq), lambda b, h, i: (b, h, 0, 0)),
            pl.BlockSpec((heads, 2 * MOBA_BLOCK, MOBA_BLOCK), lambda b, h, i: (h, 0, 0)),
        ] + [chunk(v) for v in views],
        out_specs=[pl.BlockSpec((None, MOBA_BLOCK, heads * HEAD_DIM), lambda b, h, i: (b, i, h))]
        + [chunk(v) for v in views]
        + ([pl.BlockSpec((None,) + zero_view[1:],
                         lambda b, h, i: ((b * n_pairs + h) * nb + i, 0, 0, 0))] if side else []),
        out_shape=[jax.ShapeDtypeStruct((bsz, seq, n_heads * HEAD_DIM), BF16)]
        + [jax.ShapeDtypeStruct(v, BF16) for v in views]
        + ([jax.ShapeDtypeStruct(zero_view, BF16)] if side else []),
        scratch_shapes=[
            pltpu.VMEM((heads, nb, HEAD_DIM), F32),
            pltpu.VMEM((heads, nb, MOBA_BLOCK), F32),
            pltpu.VMEM((heads, 1, MOBA_BLOCK), F32),
            pltpu.VMEM((heads, 1, MOBA_BLOCK), F32),
            pltpu.VMEM((heads, HEAD_DIM, MOBA_BLOCK), F32),
        ],
        compiler_params=_params("parallel", "parallel", "arbitrary"),
        name="moba_attention",
    )(kvq, kvq, vt, tables, *side)
    if not side:
        return (out[0], [w.astype(BF16) for w in later_weights], jnp.zeros(zero_shape, BF16))
    weights = [o.reshape(w.shape) for o, w in zip(out[1:-1], later_weights)]
    return out[0], weights, out[-1].reshape(zero_shape)


def _oproj_kernel(a_ref, w_ref, x_ref, gate_ref, ng_ref, sh_ref, sc_ref, wr_ref,
                  x3_ref, h_ref, lg_ref, xs_ref, *, nj, tn):
    j = pl.program_id(1)
    y = jnp.dot(a_ref[...], w_ref[...], preferred_element_type=F32)
    x3 = x_ref[...] + gate_ref[...] * y
    x3_ref[...] = x3
    xs_ref[j] = x3

    @pl.when(j == nj - 1)
    def _():
        d = nj * tn
        ss = jnp.zeros((x3.shape[0], 1), F32)
        for jj in range(nj):
            xj = xs_ref[jj]
            ss = ss + jnp.sum(xj * xj, axis=-1, keepdims=True)
        inv = lax.rsqrt(ss / d + EPS)
        logits = jnp.zeros(lg_ref.shape, F32)
        for jj in range(nj):
            cols = slice(jj * tn, (jj + 1) * tn)
            hj = (xs_ref[jj] * inv * ng_ref[:, cols]) * (1.0 + sc_ref[:, cols]) + sh_ref[:, cols]
            hi = hj.astype(BF16)
            lo = (hj - hi.astype(F32)).astype(BF16)
            h_ref[:, cols] = hi
            w = wr_ref[cols, :]
            whi = w.astype(BF16)
            wlo = (w - whi.astype(F32)).astype(BF16)
            logits = (logits + jnp.dot(hi, whi, preferred_element_type=F32)
                      + (jnp.dot(hi, wlo, preferred_element_type=F32)
                         + jnp.dot(lo, whi, preferred_element_type=F32)))
        lg_ref[...] = logits


def _oproj_layer(attn, x, gate, norm_g, sh, sc, w_o, router_w):
    bsz, seq, d = x.shape
    n_exp = router_w.shape[1]
    assert n_exp <= LANES
    wr = jnp.pad(router_w, ((0, 0), (0, LANES - n_exp)))
    tm = _tile(seq, 512)
    tn = _tile(d, 512)
    nj = d // tn
    per_b = seq // tm
    tok = bsz * seq
    vec_t = pl.BlockSpec((None, 1, tn), lambda i, j: (i // per_b, 0, j))
    vec = pl.BlockSpec((None, 1, d), lambda i, j: (i // per_b, 0, 0))
    x3, h, logits = pl.pallas_call(
        functools.partial(_oproj_kernel, nj=nj, tn=tn),
        grid=(bsz * per_b, nj),
        in_specs=[
            pl.BlockSpec((tm, d), lambda i, j: (i, 0)),
            pl.BlockSpec((d, tn), lambda i, j: (0, j)),
            pl.BlockSpec((tm, tn), lambda i, j: (i, j)),
            vec_t,
            pl.BlockSpec((1, d), lambda i, j: (0, 0)),
            vec, vec,
            pl.BlockSpec((d, LANES), lambda i, j: (0, 0)),
        ],
        out_specs=[
            pl.BlockSpec((tm, tn), lambda i, j: (i, j)),
            pl.BlockSpec((tm, d), lambda i, j: (i, 0)),
            pl.BlockSpec((tm, LANES), lambda i, j: (i, 0)),
        ],
        out_shape=[
            jax.ShapeDtypeStruct((tok, d), F32),
            jax.ShapeDtypeStruct((tok, d), BF16),
            jax.ShapeDtypeStruct((tok, LANES), F32),
        ],
        scratch_shapes=[pltpu.VMEM((nj, tm, tn), F32)],
        compiler_params=_params("parallel", "arbitrary"),
        name="oproj_prenorm_router",
    )(attn.reshape(tok, d), w_o.astype(BF16), x.reshape(tok, d), gate, norm_g.reshape(1, d),
      sh, sc, wr)
    return x3, h, logits


ROUTE_E0, ROUTE_E1, ROUTE_W0, ROUTE_W1, ROUTE_R0, ROUTE_R1 = range(6)


def _route_kernel(lg_ref, rb_ref, route_ref, cnt_ref, carry_ref, *, n_exp):
    step = pl.program_id(0)
    tr = lg_ref.shape[0]

    @pl.when(step == 0)
    def _():
        carry_ref[...] = jnp.zeros_like(carry_ref)

    lane = lax.broadcasted_iota(jnp.int32, (tr, LANES), 1)
    lanef = lane.astype(F32)
    logits = jnp.where(lane < n_exp, lg_ref[...] + rb_ref[...], -jnp.inf)
    m0 = jnp.max(logits, axis=1, keepdims=True)
    i0 = jnp.min(jnp.where(logits == m0, lanef, float(LANES)), axis=1, keepdims=True)
    oh0 = lanef == i0
    rest = jnp.where(oh0, -jnp.inf, logits)
    m1 = jnp.max(rest, axis=1, keepdims=True)
    i1 = jnp.min(jnp.where(rest == m1, lanef, float(LANES)), axis=1, keepdims=True)
    oh1 = lanef == i1
    e1 = jnp.exp(m1 - m0)
    w0 = 1.0 / (1.0 + e1)
    w1 = e1 / (1.0 + e1)
    oh = jnp.where(oh0 | oh1, 1.0, 0.0)
    r = lax.broadcasted_iota(jnp.int32, (tr, tr), 0)
    c = lax.broadcasted_iota(jnp.int32, (tr, tr), 1)
    earlier = jnp.where(c < r, 1.0, 0.0).astype(BF16)
    base = carry_ref[...] + jnp.dot(earlier, oh.astype(BF16), preferred_element_type=F32)
    r0 = jnp.sum(jnp.where(oh0, base, 0.0), axis=1, keepdims=True)
    r1 = jnp.sum(jnp.where(oh1, base, 0.0), axis=1, keepdims=True)
    total = carry_ref[...] + jnp.sum(oh, axis=0, keepdims=True)
    carry_ref[...] = total
    cnt_ref[...] = jnp.broadcast_to(total, cnt_ref.shape)
    out = jnp.zeros((tr, LANES), F32)
    for slot, val in ((ROUTE_E0, i0), (ROUTE_E1, i1), (ROUTE_W0, w0),
                      (ROUTE_W1, w1), (ROUTE_R0, r0), (ROUTE_R1, r1)):
        out = jnp.where(lane == slot, val, out)
    route_ref[...] = out


def _route(logits, router_b):
    tok = logits.shape[0]
    n_exp = router_b.shape[0]
    tr = _tile(tok, 512)
    rb = jnp.pad(router_b.astype(F32), (0, LANES - n_exp)).reshape(1, LANES)
    return pl.pallas_call(
        functools.partial(_route_kernel, n_exp=n_exp),
        grid=(tok // tr,),
        in_specs=[pl.BlockSpec((tr, LANES), lambda s: (s, 0)),
                  pl.BlockSpec((1, LANES), lambda s: (0, 0))],
        out_specs=[pl.BlockSpec((tr, LANES), lambda s: (s, 0)),
                   pl.BlockSpec((SUBLANES, LANES), lambda s: (0, 0))],
        out_shape=[jax.ShapeDtypeStruct((tok, LANES), F32),
                   jax.ShapeDtypeStruct((SUBLANES, LANES), F32)],
        scratch_shapes=[pltpu.VMEM((1, LANES), F32)],
        compiler_params=_params("arbitrary"),
        name="route_top2",
    )(logits, rb)


def _row_copy_kernel(start_ref, idx_ref, src_ref, *rest, td, gather):
    dst_ref, sem = rest[-2:]

    def copy(r, k):
        pos = start_ref[idx_ref[k, r]] + idx_ref[TOP_K + k, r]
        if gather:
            return pltpu.make_async_copy(src_ref.at[pos], dst_ref.at[k, r], sem)
        return pltpu.make_async_copy(src_ref.at[r], dst_ref.at[pos], sem)

    def issue(r, carry):
        for k in range(TOP_K):
            copy(r, k).start()
        return carry

    def drain(r, carry):
        for k in range(TOP_K):
            copy(r, k).wait()
        return carry

    lax.fori_loop(0, td, issue, 0)
    lax.fori_loop(0, td, drain, 0)


def _row_copy(src3, idx, row_start, scatter_into=None):
    n_steps, _, td = idx.shape
    slab = src3.shape[1:]
    gather = scatter_into is None
    any_spec = pl.BlockSpec(memory_space=pl.ANY)
    idx_spec = pl.BlockSpec((None, 2 * TOP_K, td), lambda s, start: (s, 0, 0),
                            memory_space=pltpu.SMEM)
    if gather:
        out_shape = jax.ShapeDtypeStruct((TOP_K, n_steps * td) + slab, src3.dtype)
        operands = [row_start, idx, src3]
        in_specs = [idx_spec, any_spec]
        out_spec = pl.BlockSpec((TOP_K, td) + slab, lambda s, start: (0, s, 0, 0))
        aliases = {}
    else:
        out_shape = jax.ShapeDtypeStruct(scatter_into.shape, src3.dtype)
        operands = [row_start, idx, src3, scatter_into]
        in_specs = [idx_spec, pl.BlockSpec((td,) + slab, lambda s, start: (s, 0, 0)), any_spec]
        out_spec = any_spec
        aliases = {3: 0}
    return pl.pallas_call(
        functools.partial(_row_copy_kernel, td=td, gather=gather),
        grid_spec=pltpu.PrefetchScalarGridSpec(
            num_scalar_prefetch=1,
            grid=(n_steps,),
            in_specs=in_specs,
            out_specs=out_spec,
            scratch_shapes=[pltpu.SemaphoreType.DMA(())],
        ),
        out_shape=out_shape,
        input_output_aliases=aliases,
        compiler_params=pltpu.CompilerParams(dimension_semantics=("arbitrary",),
                                             vmem_limit_bytes=VMEM_LIMIT,
                                             has_side_effects=True),
        name="moe_gather" if gather else "moe_dispatch",
    )(*operands)


def _moe_gateup_kernel(te_ref, na_ref, h_ref, w1_ref, w3_ref, o_ref):
    active = pl.program_id(0) < na_ref[0]

    @pl.when(active)
    def _():
        h = h_ref[...]
        a = jnp.dot(h, w1_ref[...], preferred_element_type=F32)
        b = jnp.dot(h, w3_ref[...], preferred_element_type=F32)
        o_ref[...] = _silu_mul(a, b).astype(BF16)

    @pl.when(jnp.logical_not(active))
    def _():
        o_ref[...] = jnp.zeros_like(o_ref)


def _moe_down_kernel(te_ref, na_ref, g_ref, w_ref, o_ref):
    active = pl.program_id(0) < na_ref[0]

    @pl.when(active)
    def _():
        o_ref[...] = jnp.dot(g_ref[...], w_ref[...], preferred_element_type=F32).astype(BF16)

    @pl.when(jnp.logical_not(active))
    def _():
        o_ref[...] = jnp.zeros_like(o_ref)


def _moe_ffn(hs, tile_expert, n_active, w1, w3, w2, *, tm):
    rows, d = hs.shape
    tn = _tile(w1.shape[-1], 512)
    w1, w3, w2 = _swiglu_weights(w1, w3, w2, tn)
    ff = w1.shape[-1]
    tn2 = _tile(d, 1024)
    n_tiles = rows // tm

    def w_map(i, j, te, na):
        return (te[i], 0, jnp.where(i < na[0], j, 0))

    w_up = pl.BlockSpec((None, d, tn), w_map)
    hidden = pl.pallas_call(
        _moe_gateup_kernel,
        grid_spec=pltpu.PrefetchScalarGridSpec(
            num_scalar_prefetch=2,
            grid=(n_tiles, ff // tn),
            in_specs=[pl.BlockSpec((tm, d), lambda i, j, te, na: (i, 0)), w_up, w_up],
            out_specs=pl.BlockSpec((tm, tn), lambda i, j, te, na: (i, j)),
        ),
        out_shape=jax.ShapeDtypeStruct((rows, ff), BF16),
        compiler_params=_params("parallel", "arbitrary"),
        name="moe_gate_up",
    )(tile_expert, n_active, hs, w1, w3)
    return pl.pallas_call(
        _moe_down_kernel,
        grid_spec=pltpu.PrefetchScalarGridSpec(
            num_scalar_prefetch=2,
            grid=(n_tiles, d // tn2),
            in_specs=[
                pl.BlockSpec((tm, ff), lambda i, j, te, na: (i, 0)),
                pl.BlockSpec((None, ff, tn2), w_map),
            ],
            out_specs=pl.BlockSpec((tm, tn2), lambda i, j, te, na: (i, j)),
        ),
        out_shape=jax.ShapeDtypeStruct((rows, d), BF16),
        compiler_params=_params("parallel", "arbitrary"),
        name="moe_down",
    )(tile_expert, n_active, hidden, w2)


def _combine_kernel(x_ref, y_ref, route_ref, gate_ref, o_ref):
    route = route_ref[...]
    lane = lax.broadcasted_iota(jnp.int32, route.shape, 1)
    w0 = jnp.sum(jnp.where(lane == ROUTE_W0, route, 0.0), axis=1, keepdims=True)
    w1 = jnp.sum(jnp.where(lane == ROUTE_W1, route, 0.0), axis=1, keepdims=True)
    y = w0 * y_ref[0].astype(F32) + w1 * y_ref[1].astype(F32)
    o_ref[...] = x_ref[...] + gate_ref[...] * y


def _combine(x3, y2, route, gate, *, seq):
    tok, d = x3.shape
    tm = _tile(seq, 256)
    per_b = seq // tm
    return pl.pallas_call(
        _combine_kernel,
        grid=(tok // tm,),
        in_specs=[
            pl.BlockSpec((tm, d), lambda i: (i, 0)),
            pl.BlockSpec((TOP_K, tm, d), lambda i: (0, i, 0)),
            pl.BlockSpec((tm, LANES), lambda i: (i, 0)),
            pl.BlockSpec((None, 1, d), lambda i: (i // per_b, 0, 0)),
        ],
        out_specs=pl.BlockSpec((tm, d), lambda i: (i, 0)),
        out_shape=jax.ShapeDtypeStruct((tok, d), F32),
        compiler_params=_params("parallel"),
        name="moe_combine",
    )(x3, y2, route, gate)


MOE_TILE = 512


def _moe_rows(tok, n_exp):
    return ((TOP_K * tok) // MOE_TILE + n_exp) * MOE_TILE


def _moe_layer(x3, h, logits, gate, router_b, w1, w3, w2, sorted_zeros, *, seq):
    tok, d = x3.shape
    n_exp = router_b.shape[0]
    slabs = d // LANES
    tm = MOE_TILE
    route, counts = _route(logits, router_b)

    cnt = counts[0, :n_exp].astype(jnp.int32)
    tiles_per = (cnt + tm - 1) // tm
    tile_end = jnp.cumsum(tiles_per)
    row_start = (tile_end - tiles_per) * tm
    n_tiles = _moe_rows(tok, n_exp) // tm
    tile_expert = jnp.minimum(
        jnp.searchsorted(tile_end, jnp.arange(n_tiles, dtype=jnp.int32), side="right"),
        n_exp - 1).astype(jnp.int32)
    n_active = tile_end[-1:].astype(jnp.int32)

    td = _tile(tok, 256)
    idx = jnp.concatenate([route[:, ROUTE_E0:ROUTE_E1 + 1], route[:, ROUTE_R0:ROUTE_R1 + 1]],
                          axis=1).astype(jnp.int32)
    idx = idx.reshape(tok // td, td, 2 * TOP_K).transpose(0, 2, 1)

    rows = n_tiles * tm
    hs3 = _row_copy(h.reshape(tok, slabs, LANES), idx, row_start, scatter_into=sorted_zeros)
    ys = _moe_ffn(hs3.reshape(rows, d), tile_expert, n_active, w1, w3, w2, tm=tm)
    y4 = _row_copy(ys.reshape(rows, slabs, LANES), idx, row_start)
    return _combine(x3, y4.reshape(TOP_K, tok, d), route, gate, seq=seq)


def kernel(x, c, ada_w, ada_b, norm_mix_g, norm_ffn_g, pool_w, pool_b, pool_scale, kv_ada_w,
           kv_ada_b, kv_norm_g, w_k, w_v, k_norm_g, w_q, q_norm_g, w_o, rel_bias, ffn_w1, ffn_w3,
           ffn_w2, router_w, router_b, moe_w1, moe_w3, moe_w2):
    bsz, seq, d = x.shape
    assert ada_w.shape[0] == 2 and pool_w.shape[0] == 1 and w_q.shape[0] == 1

    mod = _adaln(c, ada_w, ada_b)
    kv_mod = _adaln(c, kv_ada_w[None], kv_ada_b[None])[0]

    def vecs(m, n):
        return [m[:, None, k * d:(k + 1) * d] for k in range(n)]

    sh_m0, sc_m0, g_m0, sh_f0, sc_f0, g_f0 = vecs(mod[0], 6)
    sh_m1, sc_m1, g_m1, sh_f1, sc_f1, g_f1 = vecs(mod[1], 6)
    sh_kv, sc_kv = vecs(kv_mod, 2)

    x1 = _pool_layer(x, sh_m0, sc_m0, g_m0, norm_mix_g[0], pool_w[0], pool_b[0], pool_scale[0])
    x2 = _ffn_layer(x1, sh_f0, sc_f0, g_f0, norm_ffn_g[0], ffn_w1[0], ffn_w3[0], ffn_w2[0])

    kvq = _kvq_proj(x2, sh_kv, sc_kv, sh_m1, sc_m1, kv_norm_g, norm_mix_g[1], k_norm_g,
                    q_norm_g[0], w_k, w_v, w_q[0])
    sorted_shape = (_moe_rows(bsz * seq, router_b.shape[-1]), d // LANES, LANES)
    attn, (e_w1, e_w3, e_w2), sorted_zeros = _moba_attention(
        kvq, _bias_tables(rel_bias), (moe_w1[0], moe_w3[0], moe_w2[0]), sorted_shape)
    x3, h, logits = _oproj_layer(attn, x2, g_m1, norm_ffn_g[1], sh_f1, sc_f1, w_o[0], router_w[0])

    out = _moe_layer(x3, h, logits, g_f1, router_b[0], e_w1, e_w3, e_w2, sorted_zeros, seq=seq)
    return out.reshape(bsz, seq, d)
```

```python
import functools
import math

import jax
import jax.numpy as jnp
from jax import lax
from jax.experimental import pallas as pl
from jax.experimental.pallas import tpu as pltpu

F32 = jnp.float32
BF16 = jnp.bfloat16

EPS = 1e-6
POOL_WINDOWS = (2, 4, 8, 16)
HEAD_DIM = 128
MOBA_BLOCK = 256
MOBA_TOPK = 3
MAX_DISTANCE = 128
TOP_K = 2
LOG2_E = math.log2(math.e)
Q_SCALE = HEAD_DIM ** -0.5 * LOG2_E

LANES = 128
SUBLANES = 8
POOL_HALO = 16
VMEM_LIMIT = 56 * 1024 * 1024


def _params(*sem):
    return pltpu.CompilerParams(dimension_semantics=sem, vmem_limit_bytes=VMEM_LIMIT)


def _tile(n, pref):
    t = min(n, pref)
    while n % t:
        t //= 2
    return t


def _rms_mod(x, g, shift, scale):
    ms = jnp.mean(x * x, axis=-1, keepdims=True)
    return (x * lax.rsqrt(ms + EPS) * g) * (1.0 + scale) + shift


def _adaln_kernel(c_ref, w_ref, b_ref, o_ref):
    c = c_ref[...]
    s = c * (1.0 / (1.0 + jnp.exp(-c)))
    w = w_ref[...]
    s_hi = s.astype(BF16)
    s_lo = (s - s_hi.astype(F32)).astype(BF16)
    w_hi = w.astype(BF16)
    w_lo = (w - w_hi.astype(F32)).astype(BF16)
    both = jnp.dot(jnp.concatenate([s_hi, s_lo], axis=0), w_hi, preferred_element_type=F32)
    rows = s.shape[0]
    o_ref[...] = (both[:rows] + both[rows:]
                  + jnp.dot(s_hi, w_lo, preferred_element_type=F32)) + b_ref[...]


def _adaln(c, w, b):
    n_layers, d, n = w.shape
    bsz = c.shape[0]
    assert bsz <= SUBLANES
    cp = jnp.pad(c, ((0, SUBLANES - bsz), (0, 0)))
    tn = _tile(n, 512)
    out = pl.pallas_call(
        _adaln_kernel,
        grid=(n_layers, n // tn),
        in_specs=[
            pl.BlockSpec((SUBLANES, d), lambda l, j: (0, 0)),
            pl.BlockSpec((None, d, tn), lambda l, j: (l, 0, j)),
            pl.BlockSpec((None, 1, tn), lambda l, j: (l, 0, j)),
        ],
        out_specs=pl.BlockSpec((None, SUBLANES, tn), lambda l, j: (l, 0, j)),
        out_shape=jax.ShapeDtypeStruct((n_layers, SUBLANES, n), F32),
        compiler_params=_params("parallel", "parallel"),
        name="adaln",
    )(cp, w, b.reshape(n_layers, 1, n))
    return out[:, :bsz, :]


def _pool_kernel(x_ref, halo_ref, sh_ref, sc_ref, gate_ref, ng_ref, w_ref, pb_ref, ls_ref,
                 o_ref, ext_ref, *, tm, group):
    i = pl.program_id(1)
    ng, sh, sc = ng_ref[...], sh_ref[...], sc_ref[...]
    x = x_ref[...]
    ext_ref[0:POOL_HALO, :] = jnp.where(i > 0, _rms_mod(halo_ref[...], ng, sh, sc), 0.0)
    ext_ref[POOL_HALO:, :] = _rms_mod(x, ng, sh, sc)
    pos = lax.broadcasted_iota(jnp.int32, (tm, 1), 0) + i * tm
    for g, win in enumerate(POOL_WINDOWS):
        cols = slice(g * group, (g + 1) * group)
        e = ext_ref[:, cols]
        s, span = e, 1
        while span < win:
            s = s + pltpu.roll(s, span, 0)
            span *= 2
        count = jnp.minimum(pos + 1, win).astype(F32)
        p = s[POOL_HALO:, :] / count - e[POOL_HALO:, :]
        y = jnp.dot(p.astype(BF16), w_ref[g], preferred_element_type=F32) + pb_ref[g]
        o_ref[:, cols] = x[:, cols] + gate_ref[:, cols] * (y * ls_ref[:, cols])


def _pool_layer(x, sh, sc, gate, norm_g, pool_w, pool_b, pool_scale):
    bsz, seq, d = x.shape
    n_groups = len(POOL_WINDOWS)
    group = d // n_groups
    tm = _tile(seq, 256)
    assert tm % POOL_HALO == 0 and max(POOL_WINDOWS) <= POOL_HALO
    vec = pl.BlockSpec((None, 1, d), lambda b, i: (b, 0, 0))
    const = pl.BlockSpec((1, d), lambda b, i: (0, 0))
    return pl.pallas_call(
        functools.partial(_pool_kernel, tm=tm, group=group),
        grid=(bsz, seq // tm),
        in_specs=[
            pl.BlockSpec((None, tm, d), lambda b, i: (b, i, 0)),
            pl.BlockSpec((None, POOL_HALO, d),
                         lambda b, i: (b, jnp.maximum(i * (tm // POOL_HALO) - 1, 0), 0)),
            vec, vec, vec, const,
            pl.BlockSpec((n_groups, group, group), lambda b, i: (0, 0, 0)),
            pl.BlockSpec((n_groups, 1, group), lambda b, i: (0, 0, 0)),
            const,
        ],
        out_specs=pl.BlockSpec((None, tm, d), lambda b, i: (b, i, 0)),
        out_shape=jax.ShapeDtypeStruct((bsz, seq, d), F32),
        scratch_shapes=[pltpu.VMEM((POOL_HALO + tm, d), F32)],
        compiler_params=_params("parallel", "parallel"),
        name="pool_mixer",
    )(x, x, sh, sc, gate, norm_g.reshape(1, d), pool_w.astype(BF16),
      pool_b.reshape(n_groups, 1, group), pool_scale.reshape(1, d))


def _silu_mul(a, b):
    return (a * (1.0 / (1.0 + jnp.exp(-a)))) * b


def _swiglu_weights(w1, w3, w2, tn):
    pad = (-w1.shape[-1]) % tn
    cols = [(0, 0)] * (w1.ndim - 1) + [(0, pad)]
    rows = [(0, 0)] * (w2.ndim - 2) + [(0, pad), (0, 0)]
    w1, w3, w2 = w1.astype(BF16), w3.astype(BF16), w2.astype(BF16)
    if pad:
        w1, w3, w2 = jnp.pad(w1, cols), jnp.pad(w3, cols), jnp.pad(w2, rows)
    return w1, w3, w2


def _gateup_kernel(x_ref, sh_ref, sc_ref, ng_ref, w1_ref, w3_ref, o_ref, h_ref, *, rows):
    @pl.when(pl.program_id(1) == 0)
    def _():
        for r in range(0, h_ref.shape[0], rows):
            h_ref[r:r + rows, :] = _rms_mod(x_ref[r:r + rows, :], ng_ref[...], sh_ref[...],
                                            sc_ref[...]).astype(BF16)

    h = h_ref[...]
    a = jnp.dot(h, w1_ref[...], preferred_element_type=F32)
    b = jnp.dot(h, w3_ref[...], preferred_element_type=F32)
    o_ref[...] = _silu_mul(a, b).astype(BF16)


def _down_kernel(g_ref, w_ref, x_ref, gate_ref, o_ref):
    o_ref[...] = x_ref[...] + gate_ref[...] * jnp.dot(g_ref[...], w_ref[...],
                                                      preferred_element_type=F32)


def _ffn_layer(x, sh, sc, gate, norm_g, w1, w3, w2):
    bsz, seq, d = x.shape
    tok = bsz * seq
    tn = 256
    w1, w3, w2 = _swiglu_weights(w1, w3, w2, tn)
    ff = w1.shape[-1]
    tm = _tile(seq, 1024)
    per_b = seq // tm
    vec = pl.BlockSpec((None, 1, d), lambda i, j: (i // per_b, 0, 0))
    w_up = pl.BlockSpec((d, tn), lambda i, j: (0, j))
    x2 = x.reshape(tok, d)
    hidden = pl.pallas_call(
        functools.partial(_gateup_kernel, rows=_tile(tm, 256)),
        grid=(tok // tm, ff // tn),
        in_specs=[
            pl.BlockSpec((tm, d), lambda i, j: (i, 0), pipeline_mode=pl.Buffered(1)),
            vec, vec,
            pl.BlockSpec((1, d), lambda i, j: (0, 0)),
            w_up, w_up,
        ],
        out_specs=pl.BlockSpec((tm, tn), lambda i, j: (i, j)),
        out_shape=jax.ShapeDtypeStruct((tok, ff), BF16),
        scratch_shapes=[pltpu.VMEM((tm, d), BF16)],
        compiler_params=_params("parallel", "arbitrary"),
        name="ffn_gate_up",
    )(x2, sh, sc, norm_g.reshape(1, d), w1, w3)

    tn2 = _tile(d, 512)
    tm2 = _tile(seq, 512)
    per_b2 = seq // tm2
    out = pl.pallas_call(
        _down_kernel,
        grid=(tok // tm2, d // tn2),
        in_specs=[
            pl.BlockSpec((tm2, ff), lambda i, j: (i, 0)),
            pl.BlockSpec((ff, tn2), lambda i, j: (0, j)),
            pl.BlockSpec((tm2, tn2), lambda i, j: (i, j)),
            pl.BlockSpec((None, 1, tn2), lambda i, j: (i // per_b2, 0, j)),
        ],
        out_specs=pl.BlockSpec((tm2, tn2), lambda i, j: (i, j)),
        out_shape=jax.ShapeDtypeStruct((tok, d), F32),
        compiler_params=_params("parallel", "arbitrary"),
        name="ffn_down",
    )(hidden, w2, x2, gate)
    return out.reshape(bsz, seq, d)


def _kvq_kernel(x_ref, shk_ref, sck_ref, shq_ref, scq_ref, ngk_ref, ngq_ref, gk_ref, gq_ref,
                w_ref, o_ref, h_ref, *, nk):
    j = pl.program_id(1)

    @pl.when(j == 0)
    def _():
        rows = _tile(x_ref.shape[0], 128)
        for r in range(0, x_ref.shape[0], rows):
            x = x_ref[r:r + rows, :]
            xn = x * lax.rsqrt(jnp.mean(x * x, axis=-1, keepdims=True) + EPS)
            h_ref[0, r:r + rows, :] = ((xn * ngk_ref[...]) * (1.0 + sck_ref[...])
                                       + shk_ref[...]).astype(BF16)
            h_ref[1, r:r + rows, :] = ((xn * ngq_ref[...]) * (1.0 + scq_ref[...])
                                       + shq_ref[...]).astype(BF16)

    kind = j // nk
    r = jnp.dot(h_ref[kind // 2], w_ref[...], preferred_element_type=F32)
    gvec = jnp.where(kind == 2, gq_ref[...] * Q_SCALE, gk_ref[...])
    for hh in range(o_ref.shape[0]):
        rh = r[:, hh * HEAD_DIM:(hh + 1) * HEAD_DIM]
        ms = jnp.mean(rh * rh, axis=-1, keepdims=True)
        normed = rh * lax.rsqrt(ms + EPS) * gvec
        o_ref[hh] = jnp.where(kind == 1, rh, normed).astype(BF16)


def _kvq_proj(x, sh_kv, sc_kv, sh_q, sc_q, ng_kv, ng_q, k_norm_g, q_norm_g, w_k, w_v, w_q):
    bsz, seq, d = x.shape
    n_heads = d // HEAD_DIM
    w = jnp.concatenate([w_k.astype(BF16), w_v.astype(BF16), w_q.astype(BF16)], axis=1)
    tm = _tile(seq, 512)
    tn = _tile(d, 1024)
    nk = d // tn
    per_b = seq // tm
    hpt = tn // HEAD_DIM
    vec = pl.BlockSpec((None, 1, d), lambda i, j: (i // per_b, 0, 0))
    const = pl.BlockSpec((1, d), lambda i, j: (0, 0))
    hvec = pl.BlockSpec((1, HEAD_DIM), lambda i, j: (0, 0))
    return pl.pallas_call(
        functools.partial(_kvq_kernel, nk=nk),
        grid=(bsz * per_b, 3 * nk),
        in_specs=[
            pl.BlockSpec((tm, d), lambda i, j: (i, 0), pipeline_mode=pl.Buffered(1)),
            vec, vec, vec, vec, const, const, hvec, hvec,
            pl.BlockSpec((d, tn), lambda i, j: (0, j)),
        ],
        out_specs=pl.BlockSpec((None, None, hpt, tm, HEAD_DIM),
                               lambda i, j: (j // nk, i // per_b, j % nk, i % per_b, 0)),
        out_shape=jax.ShapeDtypeStruct((3, bsz, n_heads, seq, HEAD_DIM), BF16),
        scratch_shapes=[pltpu.VMEM((2, tm, d), BF16)],
        compiler_params=_params("parallel", "arbitrary"),
        name="kvq_proj",
    )(x.reshape(bsz * seq, d), sh_kv, sc_kv, sh_q, sc_q, ng_kv.reshape(1, d), ng_q.reshape(1, d),
      k_norm_g.reshape(1, HEAD_DIM), q_norm_g.reshape(1, HEAD_DIM), w)


def _bias_kernel(rb_ref, o_ref, *, n_buckets):
    h = pl.program_id(0)
    blk = o_ref.shape[-1]
    key = lax.broadcasted_iota(jnp.int32, (2 * blk, blk), 0)
    qry = lax.broadcasted_iota(jnp.int32, (2 * blk, blk), 1)
    rel = qry - key + blk
    max_exact = n_buckets // 2
    far = rb_ref[h, n_buckets - 1]
    n = jnp.maximum(rel, 0)
    nf = jnp.maximum(n, 1).astype(F32)
    large = max_exact + (jnp.log(nf / max_exact) / math.log(MAX_DISTANCE / max_exact)
                         * (n_buckets - max_exact)).astype(jnp.int32)
    large = jnp.minimum(large, n_buckets - 1)
    bucket = jnp.where(n < max_exact, n, large)
    t = jnp.zeros((2 * blk, blk), F32)
    for k in range(n_buckets):
        t = jnp.where(bucket == k, (rb_ref[h, k] - far) * LOG2_E, t)
    o_ref[...] = jnp.where(rel >= 0, t, -jnp.inf)


def _bias_tables(rel_bias):
    n_buckets, n_heads = rel_bias.shape
    return pl.pallas_call(
        functools.partial(_bias_kernel, n_buckets=n_buckets),
        grid=(n_heads,),
        in_specs=[pl.BlockSpec(memory_space=pltpu.SMEM)],
        out_specs=pl.BlockSpec((None, 2 * MOBA_BLOCK, MOBA_BLOCK), lambda h: (h, 0, 0)),
        out_shape=jax.ShapeDtypeStruct((n_heads, 2 * MOBA_BLOCK, MOBA_BLOCK), F32),
        compiler_params=_params("parallel"),
        name="bias_tables",
    )(rel_bias.T.astype(F32))


def _attn_kernel(q_ref, k_ref, vt_ref, tab_ref, *rest, nb, heads, n_side):
    side_in = rest[:n_side]
    o_ref = rest[n_side]
    side_out = rest[n_side + 1:2 * n_side + 1]
    kmean_ref, chosen_ref, m_ref, l_ref, acc_ref = rest[-5:]
    for src, dst in zip(side_in, side_out):
        dst[...] = src[...].astype(BF16)
    if n_side:
        zero_ref = rest[2 * n_side + 1]
        zero_ref[...] = jnp.zeros_like(zero_ref)

    i = pl.program_id(2)
    blk = MOBA_BLOCK
    pair = 2 * blk
    nt = (((1,), (1,)), ((), ()))

    @pl.when(i == 0)
    def _():
        for hh in range(heads):
            for jb in range(nb):
                kb = k_ref[hh, jb * blk:(jb + 1) * blk, :].astype(F32)
                kmean_ref[hh, jb:jb + 1, :] = jnp.mean(kb, axis=0, keepdims=True)

    row = lax.broadcasted_iota(jnp.int32, (nb, blk), 0)
    rowf = row.astype(F32)
    past = row < i
    prev = pl.multiple_of(jnp.maximum(i - 1, 0) * blk, blk)
    own = pl.multiple_of(i * blk, blk)
    qs = [q_ref[hh] for hh in range(heads)]

    def partial_softmax(blocks, sels):
        ms = [jnp.max(s, axis=0, keepdims=True) for s in blocks]
        ms = [m if sel is None else jnp.where(sel, m, -jnp.inf) for m, sel in zip(ms, sels)]
        m = functools.reduce(jnp.maximum, ms)
        m_safe = jnp.where(m == -jnp.inf, 0.0, m)
        subs = [m_safe if sel is None else jnp.where(sel, m_safe, jnp.inf) for sel in sels]
        ps = [jnp.exp2(s - sub) for s, sub in zip(blocks, subs)]
        total = functools.reduce(jnp.add, [jnp.sum(p, axis=0, keepdims=True) for p in ps])
        return m, total, jnp.concatenate([p.astype(BF16) for p in ps], axis=0)


    first_scores = []
    for hh in range(heads):
        q = qs[hh]
        gate = lax.dot_general(kmean_ref[hh], q.astype(F32), nt, preferred_element_type=F32,
                               precision=lax.Precision.HIGHEST)
        g = jnp.where(past, gate, -jnp.inf)
        chosen = jnp.zeros((nb, blk), F32)
        for _ in range(MOBA_TOPK):
            top = jnp.max(g, axis=0, keepdims=True)
            idx = jnp.min(jnp.where(g == top, rowf, float(nb)), axis=0, keepdims=True)
            pick = rowf == idx
            chosen = jnp.where(pick, 1.0, chosen)
            g = jnp.where(pick, -jnp.inf, g)
        chosen = jnp.where(past, chosen, 0.0)
        sel_prev = jnp.sum(jnp.where(row == i - 1, chosen, 0.0), axis=0, keepdims=True)
        chosen_ref[hh] = jnp.where(row >= i - 1, 0.0, chosen)
        kk = jnp.concatenate([k_ref[hh, pl.ds(prev, blk), :], k_ref[hh, pl.ds(own, blk), :]],
                             axis=0)
        s = lax.dot_general(kk, q, nt, preferred_element_type=F32) + tab_ref[hh]
        first_scores.append(([s[:blk, :], s[blk:, :]], [sel_prev > 0.0, None]))
    def far_scores(hh, j0):
        start = pl.multiple_of(j0 * blk, pair)
        s = lax.dot_general(k_ref[hh, pl.ds(start, pair), :], qs[hh], nt,
                            preferred_element_type=F32)
        return ([s[n * blk:(n + 1) * blk, :] for n in range(2)],
                [chosen_ref[hh, pl.ds(j0 + n, 1), :] > 0.0 for n in range(2)])

    def quad_groups(t):
        return [(hh, 4 * t + half) for half in (0, 2) for hh in range(heads)]

    first_parts = [partial_softmax(*scored) for scored in first_scores]
    for hh in range(heads):
        m, l, p = first_parts[hh]
        vv = jnp.concatenate([vt_ref[hh, :, pl.ds(prev, blk)], vt_ref[hh, :, pl.ds(own, blk)]],
                             axis=1)
        m_ref[hh] = m
        l_ref[hh] = l
        acc_ref[hh] = jnp.dot(vv, p, preferred_element_type=F32)

    def far_quad(t, carry):
        groups = quad_groups(t)
        parts = [partial_softmax(*far_scores(hh, j0)) for hh, j0 in groups]
        outs = [jnp.dot(vt_ref[hh, :, pl.ds(pl.multiple_of(j0 * blk, pair), pair)], p,
                        preferred_element_type=F32)
                for (hh, j0), (_, _, p) in zip(groups, parts)]
        for hh in range(heads):
            mine = [n for n, (gh, _) in enumerate(groups) if gh == hh]
            m_old = m_ref[hh]
            m_new = m_old
            for n in mine:
                m_new = jnp.maximum(m_new, parts[n][0])
            w_old = jnp.exp2(m_old - m_new)
            l = w_old * l_ref[hh]
            acc = w_old * acc_ref[hh]
            for n in mine:
                w = jnp.exp2(parts[n][0] - m_new)
                l = l + w * parts[n][1]
                acc = acc + w * outs[n]
            m_ref[hh] = m_new
            l_ref[hh] = l
            acc_ref[hh] = acc
        return carry

    lax.fori_loop(0, (i + 2) // 4, far_quad, 0)
    for hh in range(heads):
        o_ref[:, hh * HEAD_DIM:(hh + 1) * HEAD_DIM] = (acc_ref[hh] / l_ref[hh]).T.astype(BF16)


def _row_chunks(shape, n_steps, sublanes):
    rows = math.prod(shape[:-1])
    if rows % n_steps or (rows // n_steps) % sublanes:
        return None
    return (n_steps, rows // n_steps, shape[-1])


def _moba_attention(kvq, tables, later_weights, zero_shape):
    _, bsz, n_heads, seq, _ = kvq.shape
    nb = seq // MOBA_BLOCK
    assert seq % MOBA_BLOCK == 0 and nb % 4 == 0
    vt = jnp.swapaxes(kvq[1], -1, -2)
    heads = next(n for n in (4, 2, 1) if n_heads % n == 0)
    n_pairs = n_heads // heads
    n_steps = bsz * n_pairs * nb
    packed_rows = 2 * SUBLANES
    views = [_row_chunks(w.shape, n_steps, packed_rows) for w in later_weights]
    zero_rows = zero_shape[0] // n_steps
    if any(v is None for v in views) or zero_shape[0] % n_steps:
        views, side = [], []
    else:
        side = [w.reshape(v) for w, v in zip(later_weights, views)]

    def chunk(shape):
        return pl.BlockSpec((None,) + tuple(shape[1:]),
                            lambda b, h, i: ((b * n_pairs + h) * nb + i, 0, 0))

    zero_view = (n_steps, zero_rows) + tuple(zero_shape[1:])
    out = pl.pallas_call(
        functools.partial(_attn_kernel, nb=nb, heads=heads, n_side=len(side)),
        grid=(bsz, n_pairs, nb),
        in_specs=[
            pl.BlockSpec((None, None, heads, MOBA_BLOCK, HEAD_DIM),
                         lambda b, h, i: (2, b, h, i, 0)),
            pl.BlockSpec((None, None, heads, seq, HEAD_DIM), lambda b, h, i: (0, b, h, 0, 0)),
            pl.BlockSpec((None, heads, HEAD_DIM, seq), lambda b, h, i: (b, h, 0, 0)),
            pl.BlockSpec((heads, 2 * MOBA_BLOCK, MOBA_BLOCK), lambda b, h, i: (h, 0, 0)),
        ] + [chunk(v) for v in views],
        out_specs=[pl.BlockSpec((None, MOBA_BLOCK, heads * HEAD_DIM), lambda b, h, i: (b, i, h))]
        + [chunk(v) for v in views]
        + ([pl.BlockSpec((None,) + zero_view[1:],
                         lambda b, h, i: ((b * n_pairs + h) * nb + i, 0, 0, 0))] if side else []),
        out_shape=[jax.ShapeDtypeStruct((bsz, seq, n_heads * HEAD_DIM), BF16)]
        + [jax.ShapeDtypeStruct(v, BF16) for v in views]
        + ([jax.ShapeDtypeStruct(zero_view, BF16)] if side else []),
        scratch_shapes=[
            pltpu.VMEM((heads, nb, HEAD_DIM), F32),
            pltpu.VMEM((heads, nb, MOBA_BLOCK), F32),
            pltpu.VMEM((heads, 1, MOBA_BLOCK), F32),
            pltpu.VMEM((heads, 1, MOBA_BLOCK), F32),
            pltpu.VMEM((heads, HEAD_DIM, MOBA_BLOCK), F32),
        ],
        compiler_params=_params("parallel", "parallel", "arbitrary"),
        name="moba_attention",
    )(kvq, kvq, vt, tables, *side)
    if not side:
        return (out[0], [w.astype(BF16) for w in later_weights], jnp.zeros(zero_shape, BF16))
    weights = [o.reshape(w.shape) for o, w in zip(out[1:-1], later_weights)]
    return out[0], weights, out[-1].reshape(zero_shape)


def _oproj_kernel(a_ref, w_ref, x_ref, gate_ref, ng_ref, sh_ref, sc_ref, wr_ref,
                  x3_ref, h_ref, lg_ref, xs_ref, *, nj, tn):
    j = pl.program_id(1)
    y = jnp.dot(a_ref[...], w_ref[...], preferred_element_type=F32)
    x3 = x_ref[...] + gate_ref[...] * y
    x3_ref[...] = x3
    xs_ref[j] = x3

    @pl.when(j == nj - 1)
    def _():
        d = nj * tn
        ss = jnp.zeros((x3.shape[0], 1), F32)
        for jj in range(nj):
            xj = xs_ref[jj]
            ss = ss + jnp.sum(xj * xj, axis=-1, keepdims=True)
        inv = lax.rsqrt(ss / d + EPS)
        logits = jnp.zeros(lg_ref.shape, F32)
        for jj in range(nj):
            cols = slice(jj * tn, (jj + 1) * tn)
            hj = (xs_ref[jj] * inv * ng_ref[:, cols]) * (1.0 + sc_ref[:, cols]) + sh_ref[:, cols]
            hi = hj.astype(BF16)
            lo = (hj - hi.astype(F32)).astype(BF16)
            h_ref[:, cols] = hi
            w = wr_ref[cols, :]
            whi = w.astype(BF16)
            wlo = (w - whi.astype(F32)).astype(BF16)
            logits = (logits + jnp.dot(hi, whi, preferred_element_type=F32)
                      + (jnp.dot(hi, wlo, preferred_element_type=F32)
                         + jnp.dot(lo, whi, preferred_element_type=F32)))
        lg_ref[...] = logits


def _oproj_layer(attn, x, gate, norm_g, sh, sc, w_o, router_w):
    bsz, seq, d = x.shape
    n_exp = router_w.shape[1]
    assert n_exp <= LANES
    wr = jnp.pad(router_w, ((0, 0), (0, LANES - n_exp)))
    tm = _tile(seq, 512)
    tn = _tile(d, 512)
    nj = d // tn
    per_b = seq // tm
    tok = bsz * seq
    vec_t = pl.BlockSpec((None, 1, tn), lambda i, j: (i // per_b, 0, j))
    vec = pl.BlockSpec((None, 1, d), lambda i, j: (i // per_b, 0, 0))
    x3, h, logits = pl.pallas_call(
        functools.partial(_oproj_kernel, nj=nj, tn=tn),
        grid=(bsz * per_b, nj),
        in_specs=[
            pl.BlockSpec((tm, d), lambda i, j: (i, 0)),
            pl.BlockSpec((d, tn), lambda i, j: (0, j)),
            pl.BlockSpec((tm, tn), lambda i, j: (i, j)),
            vec_t,
            pl.BlockSpec((1, d), lambda i, j: (0, 0)),
            vec, vec,
            pl.BlockSpec((d, LANES), lambda i, j: (0, 0)),
        ],
        out_specs=[
            pl.BlockSpec((tm, tn), lambda i, j: (i, j)),
            pl.BlockSpec((tm, d), lambda i, j: (i, 0)),
            pl.BlockSpec((tm, LANES), lambda i, j: (i, 0)),
        ],
        out_shape=[
            jax.ShapeDtypeStruct((tok, d), F32),
            jax.ShapeDtypeStruct((tok, d), BF16),
            jax.ShapeDtypeStruct((tok, LANES), F32),
        ],
        scratch_shapes=[pltpu.VMEM((nj, tm, tn), F32)],
        compiler_params=_params("parallel", "arbitrary"),
        name="oproj_prenorm_router",
    )(attn.reshape(tok, d), w_o.astype(BF16), x.reshape(tok, d), gate, norm_g.reshape(1, d),
      sh, sc, wr)
    return x3, h, logits


ROUTE_E0, ROUTE_E1, ROUTE_W0, ROUTE_W1, ROUTE_R0, ROUTE_R1 = range(6)


def _route_kernel(lg_ref, rb_ref, route_ref, cnt_ref, carry_ref, *, n_exp):
    step = pl.program_id(0)
    tr = lg_ref.shape[0]

    @pl.when(step == 0)
    def _():
        carry_ref[...] = jnp.zeros_like(carry_ref)

    lane = lax.broadcasted_iota(jnp.int32, (tr, LANES), 1)
    lanef = lane.astype(F32)
    logits = jnp.where(lane < n_exp, lg_ref[...] + rb_ref[...], -jnp.inf)
    m0 = jnp.max(logits, axis=1, keepdims=True)
    i0 = jnp.min(jnp.where(logits == m0, lanef, float(LANES)), axis=1, keepdims=True)
    oh0 = lanef == i0
    rest = jnp.where(oh0, -jnp.inf, logits)
    m1 = jnp.max(rest, axis=1, keepdims=True)
    i1 = jnp.min(jnp.where(rest == m1, lanef, float(LANES)), axis=1, keepdims=True)
    oh1 = lanef == i1
    e1 = jnp.exp(m1 - m0)
    w0 = 1.0 / (1.0 + e1)
    w1 = e1 / (1.0 + e1)
    oh = jnp.where(oh0 | oh1, 1.0, 0.0)
    r = lax.broadcasted_iota(jnp.int32, (tr, tr), 0)
    c = lax.broadcasted_iota(jnp.int32, (tr, tr), 1)
    earlier = jnp.where(c < r, 1.0, 0.0).astype(BF16)
    base = carry_ref[...] + jnp.dot(earlier, oh.astype(BF16), preferred_element_type=F32)
    r0 = jnp.sum(jnp.where(oh0, base, 0.0), axis=1, keepdims=True)
    r1 = jnp.sum(jnp.where(oh1, base, 0.0), axis=1, keepdims=True)
    total = carry_ref[...] + jnp.sum(oh, axis=0, keepdims=True)
    carry_ref[...] = total
    cnt_ref[...] = jnp.broadcast_to(total, cnt_ref.shape)
    out = jnp.zeros((tr, LANES), F32)
    for slot, val in ((ROUTE_E0, i0), (ROUTE_E1, i1), (ROUTE_W0, w0),
                      (ROUTE_W1, w1), (ROUTE_R0, r0), (ROUTE_R1, r1)):
        out = jnp.where(lane == slot, val, out)
    route_ref[...] = out


def _route(logits, router_b):
    tok = logits.shape[0]
    n_exp = router_b.shape[0]
    tr = _tile(tok, 512)
    rb = jnp.pad(router_b.astype(F32), (0, LANES - n_exp)).reshape(1, LANES)
    return pl.pallas_call(
        functools.partial(_route_kernel, n_exp=n_exp),
        grid=(tok // tr,),
        in_specs=[pl.BlockSpec((tr, LANES), lambda s: (s, 0)),
                  pl.BlockSpec((1, LANES), lambda s: (0, 0))],
        out_specs=[pl.BlockSpec((tr, LANES), lambda s: (s, 0)),
                   pl.BlockSpec((SUBLANES, LANES), lambda s: (0, 0))],
        out_shape=[jax.ShapeDtypeStruct((tok, LANES), F32),
                   jax.ShapeDtypeStruct((SUBLANES, LANES), F32)],
        scratch_shapes=[pltpu.VMEM((1, LANES), F32)],
        compiler_params=_params("arbitrary"),
        name="route_top2",
    )(logits, rb)


def _row_copy_kernel(start_ref, idx_ref, src_ref, *rest, td, gather):
    dst_ref, sem = rest[-2:]

    def copy(r, k):
        pos = start_ref[idx_ref[k, r]] + idx_ref[TOP_K + k, r]
        if gather:
            return pltpu.make_async_copy(src_ref.at[pos], dst_ref.at[k, r], sem)
        return pltpu.make_async_copy(src_ref.at[r], dst_ref.at[pos], sem)

    def issue(r, carry):
        for k in range(TOP_K):
            copy(r, k).start()
        return carry

    def drain(r, carry):
        for k in range(TOP_K):
            copy(r, k).wait()
        return carry

    lax.fori_loop(0, td, issue, 0)
    lax.fori_loop(0, td, drain, 0)


def _row_copy(src3, idx, row_start, scatter_into=None):
    n_steps, _, td = idx.shape
    slab = src3.shape[1:]
    gather = scatter_into is None
    any_spec = pl.BlockSpec(memory_space=pl.ANY)
    idx_spec = pl.BlockSpec((None, 2 * TOP_K, td), lambda s, start: (s, 0, 0),
                            memory_space=pltpu.SMEM)
    if gather:
        out_shape = jax.ShapeDtypeStruct((TOP_K, n_steps * td) + slab, src3.dtype)
        operands = [row_start, idx, src3]
        in_specs = [idx_spec, any_spec]
        out_spec = pl.BlockSpec((TOP_K, td) + slab, lambda s, start: (0, s, 0, 0))
        aliases = {}
    else:
        out_shape = jax.ShapeDtypeStruct(scatter_into.shape, src3.dtype)
        operands = [row_start, idx, src3, scatter_into]
        in_specs = [idx_spec, pl.BlockSpec((td,) + slab, lambda s, start: (s, 0, 0)), any_spec]
        out_spec = any_spec
        aliases = {3: 0}
    return pl.pallas_call(
        functools.partial(_row_copy_kernel, td=td, gather=gather),
        grid_spec=pltpu.PrefetchScalarGridSpec(
            num_scalar_prefetch=1,
            grid=(n_steps,),
            in_specs=in_specs,
            out_specs=out_spec,
            scratch_shapes=[pltpu.SemaphoreType.DMA(())],
        ),
        out_shape=out_shape,
        input_output_aliases=aliases,
        compiler_params=pltpu.CompilerParams(dimension_semantics=("arbitrary",),
                                             vmem_limit_bytes=VMEM_LIMIT,
                                             has_side_effects=True),
        name="moe_gather" if gather else "moe_dispatch",
    )(*operands)


def _moe_gateup_kernel(te_ref, na_ref, h_ref, w1_ref, w3_ref, o_ref):
    active = pl.program_id(0) < na_ref[0]

    @pl.when(active)
    def _():
        h = h_ref[...]
        a = jnp.dot(h, w1_ref[...], preferred_element_type=F32)
        b = jnp.dot(h, w3_ref[...], preferred_element_type=F32)
        o_ref[...] = _silu_mul(a, b).astype(BF16)

    @pl.when(jnp.logical_not(active))
    def _():
        o_ref[...] = jnp.zeros_like(o_ref)


def _moe_down_kernel(te_ref, na_ref, g_ref, w_ref, o_ref):
    active = pl.program_id(0) < na_ref[0]

    @pl.when(active)
    def _():
        o_ref[...] = jnp.dot(g_ref[...], w_ref[...], preferred_element_type=F32).astype(BF16)

    @pl.when(jnp.logical_not(active))
    def _():
        o_ref[...] = jnp.zeros_like(o_ref)


def _moe_ffn(hs, tile_expert, n_active, w1, w3, w2, *, tm):
    rows, d = hs.shape
    tn = _tile(w1.shape[-1], 512)
    w1, w3, w2 = _swiglu_weights(w1, w3, w2, tn)
    ff = w1.shape[-1]
    tn2 = _tile(d, 1024)
    n_tiles = rows // tm

    def w_map(i, j, te, na):
        return (te[i], 0, jnp.where(i < na[0], j, 0))

    w_up = pl.BlockSpec((None, d, tn), w_map)
    hidden = pl.pallas_call(
        _moe_gateup_kernel,
        grid_spec=pltpu.PrefetchScalarGridSpec(
            num_scalar_prefetch=2,
            grid=(n_tiles, ff // tn),
            in_specs=[pl.BlockSpec((tm, d), lambda i, j, te, na: (i, 0)), w_up, w_up],
            out_specs=pl.BlockSpec((tm, tn), lambda i, j, te, na: (i, j)),
        ),
        out_shape=jax.ShapeDtypeStruct((rows, ff), BF16),
        compiler_params=_params("parallel", "arbitrary"),
        name="moe_gate_up",
    )(tile_expert, n_active, hs, w1, w3)
    return pl.pallas_call(
        _moe_down_kernel,
        grid_spec=pltpu.PrefetchScalarGridSpec(
            num_scalar_prefetch=2,
            grid=(n_tiles, d // tn2),
            in_specs=[
                pl.BlockSpec((tm, ff), lambda i, j, te, na: (i, 0)),
                pl.BlockSpec((None, ff, tn2), w_map),
            ],
            out_specs=pl.BlockSpec((tm, tn2), lambda i, j, te, na: (i, j)),
        ),
        out_shape=jax.ShapeDtypeStruct((rows, d), BF16),
        compiler_params=_params("parallel", "arbitrary"),
        name="moe_down",
    )(tile_expert, n_active, hidden, w2)


def _combine_kernel(x_ref, y_ref, route_ref, gate_ref, o_ref):
    route = route_ref[...]
    lane = lax.broadcasted_iota(jnp.int32, route.shape, 1)
    w0 = jnp.sum(jnp.where(lane == ROUTE_W0, route, 0.0), axis=1, keepdims=True)
    w1 = jnp.sum(jnp.where(lane == ROUTE_W1, route, 0.0), axis=1, keepdims=True)
    y = w0 * y_ref[0].astype(F32) + w1 * y_ref[1].astype(F32)
    o_ref[...] = x_ref[...] + gate_ref[...] * y


def _combine(x3, y2, route, gate, *, seq):
    tok, d = x3.shape
    tm = _tile(seq, 256)
    per_b = seq // tm
    return pl.pallas_call(
        _combine_kernel,
        grid=(tok // tm,),
        in_specs=[
            pl.BlockSpec((tm, d), lambda i: (i, 0)),
            pl.BlockSpec((TOP_K, tm, d), lambda i: (0, i, 0)),
            pl.BlockSpec((tm, LANES), lambda i: (i, 0)),
            pl.BlockSpec((None, 1, d), lambda i: (i // per_b, 0, 0)),
        ],
        out_specs=pl.BlockSpec((tm, d), lambda i: (i, 0)),
        out_shape=jax.ShapeDtypeStruct((tok, d), F32),
        compiler_params=_params("parallel"),
        name="moe_combine",
    )(x3, y2, route, gate)


MOE_TILE = 512


def _moe_rows(tok, n_exp):
    return ((TOP_K * tok) // MOE_TILE + n_exp) * MOE_TILE


def _moe_layer(x3, h, logits, gate, router_b, w1, w3, w2, sorted_zeros, *, seq):
    tok, d = x3.shape
    n_exp = router_b.shape[0]
    slabs = d // LANES
    tm = MOE_TILE
    route, counts = _route(logits, router_b)

    cnt = counts[0, :n_exp].astype(jnp.int32)
    tiles_per = (cnt + tm - 1) // tm
    tile_end = jnp.cumsum(tiles_per)
    row_start = (tile_end - tiles_per) * tm
    n_tiles = _moe_rows(tok, n_exp) // tm
    tile_expert = jnp.minimum(
        jnp.searchsorted(tile_end, jnp.arange(n_tiles, dtype=jnp.int32), side="right"),
        n_exp - 1).astype(jnp.int32)
    n_active = tile_end[-1:].astype(jnp.int32)

    td = _tile(tok, 256)
    idx = jnp.concatenate([route[:, ROUTE_E0:ROUTE_E1 + 1], route[:, ROUTE_R0:ROUTE_R1 + 1]],
                          axis=1).astype(jnp.int32)
    idx = idx.reshape(tok // td, td, 2 * TOP_K).transpose(0, 2, 1)

    rows = n_tiles * tm
    hs3 = _row_copy(h.reshape(tok, slabs, LANES), idx, row_start, scatter_into=sorted_zeros)
    ys = _moe_ffn(hs3.reshape(rows, d), tile_expert, n_active, w1, w3, w2, tm=tm)
    y4 = _row_copy(ys.reshape(rows, slabs, LANES), idx, row_start)
    return _combine(x3, y4.reshape(TOP_K, tok, d), route, gate, seq=seq)


def kernel(x, c, ada_w, ada_b, norm_mix_g, norm_ffn_g, pool_w, pool_b, pool_scale, kv_ada_w,
           kv_ada_b, kv_norm_g, w_k, w_v, k_norm_g, w_q, q_norm_g, w_o, rel_bias, ffn_w1, ffn_w3,
           ffn_w2, router_w, router_b, moe_w1, moe_w3, moe_w2):
    bsz, seq, d = x.shape
    assert ada_w.shape[0] == 2 and pool_w.shape[0] == 1 and w_q.shape[0] == 1

    mod = _adaln(c, ada_w, ada_b)
    kv_mod = _adaln(c, kv_ada_w[None], kv_ada_b[None])[0]

    def vecs(m, n):
        return [m[:, None, k * d:(k + 1) * d] for k in range(n)]

    sh_m0, sc_m0, g_m0, sh_f0, sc_f0, g_f0 = vecs(mod[0], 6)
    sh_m1, sc_m1, g_m1, sh_f1, sc_f1, g_f1 = vecs(mod[1], 6)
    sh_kv, sc_kv = vecs(kv_mod, 2)

    x1 = _pool_layer(x, sh_m0, sc_m0, g_m0, norm_mix_g[0], pool_w[0], pool_b[0], pool_scale[0])
    x2 = _ffn_layer(x1, sh_f0, sc_f0, g_f0, norm_ffn_g[0], ffn_w1[0], ffn_w3[0], ffn_w2[0])

    kvq = _kvq_proj(x2, sh_kv, sc_kv, sh_m1, sc_m1, kv_norm_g, norm_mix_g[1], k_norm_g,
                    q_norm_g[0], w_k, w_v, w_q[0])
    sorted_shape = (_moe_rows(bsz * seq, router_b.shape[-1]), d // LANES, LANES)
    attn, (e_w1, e_w3, e_w2), sorted_zeros = _moba_attention(
        kvq, _bias_tables(rel_bias), (moe_w1[0], moe_w3[0], moe_w2[0]), sorted_shape)
    x3, h, logits = _oproj_layer(attn, x2, g_m1, norm_ffn_g[1], sh_f1, sc_f1, w_o[0], router_w[0])

    out = _moe_layer(x3, h, logits, g_f1, router_b[0], e_w1, e_w3, e_w2, sorted_zeros, seq=seq)
    return out.reshape(bsz, seq, d)
```

```python
import functools
import math

import jax
import jax.numpy as jnp
from jax import lax
from jax.experimental import pallas as pl
from jax.experimental.pallas import tpu as pltpu

F32 = jnp.float32
BF16 = jnp.bfloat16

EPS = 1e-6
POOL_WINDOWS = (2, 4, 8, 16)
HEAD_DIM = 128
MOBA_BLOCK = 256
MOBA_TOPK = 3
MAX_DISTANCE = 128
TOP_K = 2
LOG2_E = math.log2(math.e)
Q_SCALE = HEAD_DIM ** -0.5 * LOG2_E

LANES = 128
SUBLANES = 8
POOL_HALO = 16
VMEM_LIMIT = 56 * 1024 * 1024


def _params(*sem):
    return pltpu.CompilerParams(dimension_semantics=sem, vmem_limit_bytes=VMEM_LIMIT)


def _tile(n, pref):
    t = min(n, pref)
    while n % t:
        t //= 2
    return t


def _rms_mod(x, g, shift, scale):
    ms = jnp.mean(x * x, axis=-1, keepdims=True)
    return (x * lax.rsqrt(ms + EPS) * g) * (1.0 + scale) + shift


def _adaln_kernel(c_ref, w_ref, b_ref, o_ref):
    c = c_ref[...]
    s = c * (1.0 / (1.0 + jnp.exp(-c)))
    w = w_ref[...]
    s_hi = s.astype(BF16)
    s_lo = (s - s_hi.astype(F32)).astype(BF16)
    w_hi = w.astype(BF16)
    w_lo = (w - w_hi.astype(F32)).astype(BF16)
    both = jnp.dot(jnp.concatenate([s_hi, s_lo], axis=0), w_hi, preferred_element_type=F32)
    rows = s.shape[0]
    o_ref[...] = (both[:rows] + both[rows:]
                  + jnp.dot(s_hi, w_lo, preferred_element_type=F32)) + b_ref[...]


def _adaln(c, w, b):
    n_layers, d, n = w.shape
    bsz = c.shape[0]
    assert bsz <= SUBLANES
    cp = jnp.pad(c, ((0, SUBLANES - bsz), (0, 0)))
    tn = _tile(n, 512)
    out = pl.pallas_call(
        _adaln_kernel,
        grid=(n_layers, n // tn),
        in_specs=[
            pl.BlockSpec((SUBLANES, d), lambda l, j: (0, 0)),
            pl.BlockSpec((None, d, tn), lambda l, j: (l, 0, j)),
            pl.BlockSpec((None, 1, tn), lambda l, j: (l, 0, j)),
        ],
        out_specs=pl.BlockSpec((None, SUBLANES, tn), lambda l, j: (l, 0, j)),
        out_shape=jax.ShapeDtypeStruct((n_layers, SUBLANES, n), F32),
        compiler_params=_params("parallel", "parallel"),
        name="adaln",
    )(cp, w, b.reshape(n_layers, 1, n))
    return out[:, :bsz, :]


def _pool_kernel(x_ref, halo_ref, sh_ref, sc_ref, gate_ref, ng_ref, w_ref, pb_ref, ls_ref,
                 o_ref, ext_ref, *, tm, group):
    i = pl.program_id(1)
    ng, sh, sc = ng_ref[...], sh_ref[...], sc_ref[...]
    x = x_ref[...]
    ext_ref[0:POOL_HALO, :] = jnp.where(i > 0, _rms_mod(halo_ref[...], ng, sh, sc), 0.0)
    ext_ref[POOL_HALO:, :] = _rms_mod(x, ng, sh, sc)
    pos = lax.broadcasted_iota(jnp.int32, (tm, 1), 0) + i * tm
    for g, win in enumerate(POOL_WINDOWS):
        cols = slice(g * group, (g + 1) * group)
        e = ext_ref[:, cols]
        s, span = e, 1
        while span < win:
            s = s + pltpu.roll(s, span, 0)
            span *= 2
        count = jnp.minimum(pos + 1, win).astype(F32)
        p = s[POOL_HALO:, :] / count - e[POOL_HALO:, :]
        y = jnp.dot(p.astype(BF16), w_ref[g], preferred_element_type=F32) + pb_ref[g]
        o_ref[:, cols] = x[:, cols] + gate_ref[:, cols] * (y * ls_ref[:, cols])


def _pool_layer(x, sh, sc, gate, norm_g, pool_w, pool_b, pool_scale):
    bsz, seq, d = x.shape
    n_groups = len(POOL_WINDOWS)
    group = d // n_groups
    tm = _tile(seq, 256)
    assert tm % POOL_HALO == 0 and max(POOL_WINDOWS) <= POOL_HALO
    vec = pl.BlockSpec((None, 1, d), lambda b, i: (b, 0, 0))
    const = pl.BlockSpec((1, d), lambda b, i: (0, 0))
    return pl.pallas_call(
        functools.partial(_pool_kernel, tm=tm, group=group),
        grid=(bsz, seq // tm),
        in_specs=[
            pl.BlockSpec((None, tm, d), lambda b, i: (b, i, 0)),
            pl.BlockSpec((None, POOL_HALO, d),
                         lambda b, i: (b, jnp.maximum(i * (tm // POOL_HALO) - 1, 0), 0)),
            vec, vec, vec, const,
            pl.BlockSpec((n_groups, group, group), lambda b, i: (0, 0, 0)),
            pl.BlockSpec((n_groups, 1, group), lambda b, i: (0, 0, 0)),
            const,
        ],
        out_specs=pl.BlockSpec((None, tm, d), lambda b, i: (b, i, 0)),
        out_shape=jax.ShapeDtypeStruct((bsz, seq, d), F32),
        scratch_shapes=[pltpu.VMEM((POOL_HALO + tm, d), F32)],
        compiler_params=_params("parallel", "parallel"),
        name="pool_mixer",
    )(x, x, sh, sc, gate, norm_g.reshape(1, d), pool_w.astype(BF16),
      pool_b.reshape(n_groups, 1, group), pool_scale.reshape(1, d))


def _silu_mul(a, b):
    return (a * (1.0 / (1.0 + jnp.exp(-a)))) * b


def _swiglu_weights(w1, w3, w2, tn):
    pad = (-w1.shape[-1]) % tn
    cols = [(0, 0)] * (w1.ndim - 1) + [(0, pad)]
    rows = [(0, 0)] * (w2.ndim - 2) + [(0, pad), (0, 0)]
    w1, w3, w2 = w1.astype(BF16), w3.astype(BF16), w2.astype(BF16)
    if pad:
        w1, w3, w2 = jnp.pad(w1, cols), jnp.pad(w3, cols), jnp.pad(w2, rows)
    return w1, w3, w2


def _gateup_kernel(x_ref, sh_ref, sc_ref, ng_ref, w1_ref, w3_ref, o_ref, h_ref, *, rows):
    @pl.when(pl.program_id(1) == 0)
    def _():
        for r in range(0, h_ref.shape[0], rows):
            h_ref[r:r + rows, :] = _rms_mod(x_ref[r:r + rows, :], ng_ref[...], sh_ref[...],
                                            sc_ref[...]).astype(BF16)

    h = h_ref[...]
    a = jnp.dot(h, w1_ref[...], preferred_element_type=F32)
    b = jnp.dot(h, w3_ref[...], preferred_element_type=F32)
    o_ref[...] = _silu_mul(a, b).astype(BF16)


def _down_kernel(g_ref, w_ref, x_ref, gate_ref, o_ref):
    o_ref[...] = x_ref[...] + gate_ref[...] * jnp.dot(g_ref[...], w_ref[...],
                                                      preferred_element_type=F32)


def _ffn_layer(x, sh, sc, gate, norm_g, w1, w3, w2):
    bsz, seq, d = x.shape
    tok = bsz * seq
    tn = 256
    w1, w3, w2 = _swiglu_weights(w1, w3, w2, tn)
    ff = w1.shape[-1]
    tm = _tile(seq, 1024)
    per_b = seq // tm
    vec = pl.BlockSpec((None, 1, d), lambda i, j: (i // per_b, 0, 0))
    w_up = pl.BlockSpec((d, tn), lambda i, j: (0, j))
    x2 = x.reshape(tok, d)
    hidden = pl.pallas_call(
        functools.partial(_gateup_kernel, rows=_tile(tm, 256)),
        grid=(tok // tm, ff // tn),
        in_specs=[
            pl.BlockSpec((tm, d), lambda i, j: (i, 0), pipeline_mode=pl.Buffered(1)),
            vec, vec,
            pl.BlockSpec((1, d), lambda i, j: (0, 0)),
            w_up, w_up,
        ],
        out_specs=pl.BlockSpec((tm, tn), lambda i, j: (i, j)),
        out_shape=jax.ShapeDtypeStruct((tok, ff), BF16),
        scratch_shapes=[pltpu.VMEM((tm, d), BF16)],
        compiler_params=_params("parallel", "arbitrary"),
        name="ffn_gate_up",
    )(x2, sh, sc, norm_g.reshape(1, d), w1, w3)

    tn2 = _tile(d, 512)
    tm2 = _tile(seq, 512)
    per_b2 = seq // tm2
    out = pl.pallas_call(
        _down_kernel,
        grid=(tok // tm2, d // tn2),
        in_specs=[
            pl.BlockSpec((tm2, ff), lambda i, j: (i, 0)),
            pl.BlockSpec((ff, tn2), lambda i, j: (0, j)),
            pl.BlockSpec((tm2, tn2), lambda i, j: (i, j)),
            pl.BlockSpec((None, 1, tn2), lambda i, j: (i // per_b2, 0, j)),
        ],
        out_specs=pl.BlockSpec((tm2, tn2), lambda i, j: (i, j)),
        out_shape=jax.ShapeDtypeStruct((tok, d), F32),
        compiler_params=_params("parallel", "arbitrary"),
        name="ffn_down",
    )(hidden, w2, x2, gate)
    return out.reshape(bsz, seq, d)


def _kvq_kernel(x_ref, shk_ref, sck_ref, shq_ref, scq_ref, ngk_ref, ngq_ref, gk_ref, gq_ref,
                w_ref, o_ref, h_ref, *, nk):
    j = pl.program_id(1)

    @pl.when(j == 0)
    def _():
        rows = _tile(x_ref.shape[0], 128)
        for r in range(0, x_ref.shape[0], rows):
            x = x_ref[r:r + rows, :]
            xn = x * lax.rsqrt(jnp.mean(x * x, axis=-1, keepdims=True) + EPS)
            h_ref[0, r:r + rows, :] = ((xn * ngk_ref[...]) * (1.0 + sck_ref[...])
                                       + shk_ref[...]).astype(BF16)
            h_ref[1, r:r + rows, :] = ((xn * ngq_ref[...]) * (1.0 + scq_ref[...])
                                       + shq_ref[...]).astype(BF16)

    kind = j // nk
    r = jnp.dot(h_ref[kind // 2], w_ref[...], preferred_element_type=F32)
    gvec = jnp.where(kind == 2, gq_ref[...] * Q_SCALE, gk_ref[...])
    for hh in range(o_ref.shape[0]):
        rh = r[:, hh * HEAD_DIM:(hh + 1) * HEAD_DIM]
        ms = jnp.mean(rh * rh, axis=-1, keepdims=True)
        normed = rh * lax.rsqrt(ms + EPS) * gvec
        o_ref[hh] = jnp.where(kind == 1, rh, normed).astype(BF16)


def _kvq_proj(x, sh_kv, sc_kv, sh_q, sc_q, ng_kv, ng_q, k_norm_g, q_norm_g, w_k, w_v, w_q):
    bsz, seq, d = x.shape
    n_heads = d // HEAD_DIM
    w = jnp.concatenate([w_k.astype(BF16), w_v.astype(BF16), w_q.astype(BF16)], axis=1)
    tm = _tile(seq, 512)
    tn = _tile(d, 1024)
    nk = d // tn
    per_b = seq // tm
    hpt = tn // HEAD_DIM
    vec = pl.BlockSpec((None, 1, d), lambda i, j: (i // per_b, 0, 0))
    const = pl.BlockSpec((1, d), lambda i, j: (0, 0))
    hvec = pl.BlockSpec((1, HEAD_DIM), lambda i, j: (0, 0))
    return pl.pallas_call(
        functools.partial(_kvq_kernel, nk=nk),
        grid=(bsz * per_b, 3 * nk),
        in_specs=[
            pl.BlockSpec((tm, d), lambda i, j: (i, 0), pipeline_mode=pl.Buffered(1)),
            vec, vec, vec, vec, const, const, hvec, hvec,
            pl.BlockSpec((d, tn), lambda i, j: (0, j)),
        ],
        out_specs=pl.BlockSpec((None, None, hpt, tm, HEAD_DIM),
                               lambda i, j: (j // nk, i // per_b, j % nk, i % per_b, 0)),
        out_shape=jax.ShapeDtypeStruct((3, bsz, n_heads, seq, HEAD_DIM), BF16),
        scratch_shapes=[pltpu.VMEM((2, tm, d), BF16)],
        compiler_params=_params("parallel", "arbitrary"),
        name="kvq_proj",
    )(x.reshape(bsz * seq, d), sh_kv, sc_kv, sh_q, sc_q, ng_kv.reshape(1, d), ng_q.reshape(1, d),
      k_norm_g.reshape(1, HEAD_DIM), q_norm_g.reshape(1, HEAD_DIM), w)


def _bias_kernel(rb_ref, o_ref, *, n_buckets):
    h = pl.program_id(0)
    blk = o_ref.shape[-1]
    key = lax.broadcasted_iota(jnp.int32, (2 * blk, blk), 0)
    qry = lax.broadcasted_iota(jnp.int32, (2 * blk, blk), 1)
    rel = qry - key + blk
    max_exact = n_buckets // 2
    far = rb_ref[h, n_buckets - 1]
    n = jnp.maximum(rel, 0)
    nf = jnp.maximum(n, 1).astype(F32)
    large = max_exact + (jnp.log(nf / max_exact) / math.log(MAX_DISTANCE / max_exact)
                         * (n_buckets - max_exact)).astype(jnp.int32)
    large = jnp.minimum(large, n_buckets - 1)
    bucket = jnp.where(n < max_exact, n, large)
    t = jnp.zeros((2 * blk, blk), F32)
    for k in range(n_buckets):
        t = jnp.where(bucket == k, (rb_ref[h, k] - far) * LOG2_E, t)
    o_ref[...] = jnp.where(rel >= 0, t, -jnp.inf)


def _bias_tables(rel_bias):
    n_buckets, n_heads = rel_bias.shape
    return pl.pallas_call(
        functools.partial(_bias_kernel, n_buckets=n_buckets),
        grid=(n_heads,),
        in_specs=[pl.BlockSpec(memory_space=pltpu.SMEM)],
        out_specs=pl.BlockSpec((None, 2 * MOBA_BLOCK, MOBA_BLOCK), lambda h: (h, 0, 0)),
        out_shape=jax.ShapeDtypeStruct((n_heads, 2 * MOBA_BLOCK, MOBA_BLOCK), F32),
        compiler_params=_params("parallel"),
        name="bias_tables",
    )(rel_bias.T.astype(F32))


def _attn_kernel(q_ref, k_ref, vt_ref, tab_ref, *rest, nb, heads, n_side):
    side_in = rest[:n_side]
    o_ref = rest[n_side]
    side_out = rest[n_side + 1:2 * n_side + 1]
    kmean_ref, chosen_ref, m_ref, l_ref, acc_ref = rest[-5:]
    for src, dst in zip(side_in, side_out):
        dst[...] = src[...].astype(BF16)
    if n_side:
        zero_ref = rest[2 * n_side + 1]
        zero_ref[...] = jnp.zeros_like(zero_ref)

    i = pl.program_id(2)
    blk = MOBA_BLOCK
    pair = 2 * blk
    nt = (((1,), (1,)), ((), ()))

    @pl.when(i == 0)
    def _():
        for hh in range(heads):
            for jb in range(nb):
                kb = k_ref[hh, jb * blk:(jb + 1) * blk, :].astype(F32)
                kmean_ref[hh, jb:jb + 1, :] = jnp.mean(kb, axis=0, keepdims=True)

    row = lax.broadcasted_iota(jnp.int32, (nb, blk), 0)
    rowf = row.astype(F32)
    past = row < i
    prev = pl.multiple_of(jnp.maximum(i - 1, 0) * blk, blk)
    own = pl.multiple_of(i * blk, blk)
    qs = [q_ref[hh] for hh in range(heads)]

    def partial_softmax(blocks, sels):
        ms = [jnp.max(s, axis=0, keepdims=True) for s in blocks]
        ms = [m if sel is None else jnp.where(sel, m, -jnp.inf) for m, sel in zip(ms, sels)]
        m = functools.reduce(jnp.maximum, ms)
        m_safe = jnp.where(m == -jnp.inf, 0.0, m)
        subs = [m_safe if sel is None else jnp.where(sel, m_safe, jnp.inf) for sel in sels]
        ps = [jnp.exp2(s - sub) for s, sub in zip(blocks, subs)]
        total = functools.reduce(jnp.add, [jnp.sum(p, axis=0, keepdims=True) for p in ps])
        return m, total, jnp.concatenate([p.astype(BF16) for p in ps], axis=0)


    first_scores = []
    for hh in range(heads):
        q = qs[hh]
        gate = lax.dot_general(kmean_ref[hh], q.astype(F32), nt, preferred_element_type=F32,
                               precision=lax.Precision.HIGHEST)
        g = jnp.where(past, gate, -jnp.inf)
        chosen = jnp.zeros((nb, blk), F32)
        for _ in range(MOBA_TOPK):
            top = jnp.max(g, axis=0, keepdims=True)
            idx = jnp.min(jnp.where(g == top, rowf, float(nb)), axis=0, keepdims=True)
            pick = rowf == idx
            chosen = jnp.where(pick, 1.0, chosen)
            g = jnp.where(pick, -jnp.inf, g)
        chosen = jnp.where(past, chosen, 0.0)
        sel_prev = jnp.sum(jnp.where(row == i - 1, chosen, 0.0), axis=0, keepdims=True)
        chosen_ref[hh] = jnp.where(row >= i - 1, 0.0, chosen)
        kk = jnp.concatenate([k_ref[hh, pl.ds(prev, blk), :], k_ref[hh, pl.ds(own, blk), :]],
                             axis=0)
        s = lax.dot_general(kk, q, nt, preferred_element_type=F32) + tab_ref[hh]
        first_scores.append(([s[:blk, :], s[blk:, :]], [sel_prev > 0.0, None]))
    def far_scores(hh, j0):
        start = pl.multiple_of(j0 * blk, pair)
        s = lax.dot_general(k_ref[hh, pl.ds(start, pair), :], qs[hh], nt,
                            preferred_element_type=F32)
        return ([s[n * blk:(n + 1) * blk, :] for n in range(2)],
                [chosen_ref[hh, pl.ds(j0 + n, 1), :] > 0.0 for n in range(2)])

    def quad_groups(t):
        return [(hh, 4 * t + half) for half in (0, 2) for hh in range(heads)]

    first_parts = [partial_softmax(*scored) for scored in first_scores]
    for hh in range(heads):
        m, l, p = first_parts[hh]
        vv = jnp.concatenate([vt_ref[hh, :, pl.ds(prev, blk)], vt_ref[hh, :, pl.ds(own, blk)]],
                             axis=1)
        m_ref[hh] = m
        l_ref[hh] = l
        acc_ref[hh] = jnp.dot(vv, p, preferred_element_type=F32)

    def far_quad(t, carry):
        groups = quad_groups(t)
        parts = [partial_softmax(*far_scores(hh, j0)) for hh, j0 in groups]
        outs = [jnp.dot(vt_ref[hh, :, pl.ds(pl.multiple_of(j0 * blk, pair), pair)], p,
                        preferred_element_type=F32)
                for (hh, j0), (_, _, p) in zip(groups, parts)]
        for hh in range(heads):
            mine = [n for n, (gh, _) in enumerate(groups) if gh == hh]
            m_old = m_ref[hh]
            m_new = m_old
            for n in mine:
                m_new = jnp.maximum(m_new, parts[n][0])
            w_old = jnp.exp2(m_old - m_new)
            l = w_old * l_ref[hh]
            acc = w_old * acc_ref[hh]
            for n in mine:
                w = jnp.exp2(parts[n][0] - m_new)
                l = l + w * parts[n][1]
                acc = acc + w * outs[n]
            m_ref[hh] = m_new
            l_ref[hh] = l
            acc_ref[hh] = acc
        return carry

    lax.fori_loop(0, (i + 2) // 4, far_quad, 0)
    for hh in range(heads):
        o_ref[:, hh * HEAD_DIM:(hh + 1) * HEAD_DIM] = (acc_ref[hh] / l_ref[hh]).T.astype(BF16)


def _row_chunks(shape, n_steps, sublanes):
    rows = math.prod(shape[:-1])
    if rows % n_steps or (rows // n_steps) % sublanes:
        return None
    return (n_steps, rows // n_steps, shape[-1])


def _moba_attention(kvq, tables, later_weights, zero_shape):
    _, bsz, n_heads, seq, _ = kvq.shape
    nb = seq // MOBA_BLOCK
    assert seq % MOBA_BLOCK == 0 and nb % 4 == 0
    vt = jnp.swapaxes(kvq[1], -1, -2)
    heads = next(n for n in (4, 2, 1) if n_heads % n == 0)
    n_pairs = n_heads // heads
    n_steps = bsz * n_pairs * nb
    packed_rows = 2 * SUBLANES
    views = [_row_chunks(w.shape, n_steps, packed_rows) for w in later_weights]
    zero_rows = zero_shape[0] // n_steps
    if any(v is None for v in views) or zero_shape[0] % n_steps:
        views, side = [], []
    else:
        side = [w.reshape(v) for w, v in zip(later_weights, views)]

    def chunk(shape):
        return pl.BlockSpec((None,) + tuple(shape[1:]),
                            lambda b, h, i: ((b * n_pairs + h) * nb + i, 0, 0))

    zero_view = (n_steps, zero_rows) + tuple(zero_shape[1:])
    out = pl.pallas_call(
        functools.partial(_attn_kernel, nb=nb, heads=heads, n_side=len(side)),
        grid=(bsz, n_pairs, nb),
        in_specs=[
            pl.BlockSpec((None, None, heads, MOBA_BLOCK, HEAD_DIM),
                         lambda b, h, i: (2, b, h, i, 0)),
            pl.BlockSpec((None, None, heads, seq, HEAD_DIM), lambda b, h, i: (0, b, h, 0, 0)),
            pl.BlockSpec((None, heads, HEAD_DIM, seq), lambda b, h, i: (b, h, 0, 0)),
            pl.BlockSpec((heads, 2 * MOBA_BLOCK, MOBA_BLOCK), lambda b, h, i: (h, 0, 0)),
        ] + [chunk(v) for v in views],
        out_specs=[pl.BlockSpec((None, MOBA_BLOCK, heads * HEAD_DIM), lambda b, h, i: (b, i, h))]
        + [chunk(v) for v in views]
        + ([pl.BlockSpec((None,) + zero_view[1:],
                         lambda b, h, i: ((b * n_pairs + h) * nb + i, 0, 0, 0))] if side else []),
        out_shape=[jax.ShapeDtypeStruct((bsz, seq, n_heads * HEAD_DIM), BF16)]
        + [jax.ShapeDtypeStruct(v, BF16) for v in views]
        + ([jax.ShapeDtypeStruct(zero_view, BF16)] if side else []),
        scratch_shapes=[
            pltpu.VMEM((heads, nb, HEAD_DIM), F32),
            pltpu.VMEM((heads, nb, MOBA_BLOCK), F32),
            pltpu.VMEM((heads, 1, MOBA_BLOCK), F32),
            pltpu.VMEM((heads, 1, MOBA_BLOCK), F32),
            pltpu.VMEM((heads, HEAD_DIM, MOBA_BLOCK), F32),
        ],
        compiler_params=_params("parallel", "parallel", "arbitrary"),
        name="moba_attention",
    )(kvq, kvq, vt, tables, *side)
    if not side:
        return (out[0], [w.astype(BF16) for w in later_weights], jnp.zeros(zero_shape, BF16))
    weights = [o.reshape(w.shape) for o, w in zip(out[1:-1], later_weights)]
    return out[0], weights, out[-1].reshape(zero_shape)


def _oproj_kernel(a_ref, w_ref, x_ref, gate_ref, ng_ref, sh_ref, sc_ref, wr_ref,
                  x3_ref, h_ref, lg_ref, xs_ref, *, nj, tn):
    j = pl.program_id(1)
    y = jnp.dot(a_ref[...], w_ref[...], preferred_element_type=F32)
    x3 = x_ref[...] + gate_ref[...] * y
    x3_ref[...] = x3
    xs_ref[j] = x3

    @pl.when(j == nj - 1)
    def _():
        d = nj * tn
        ss = jnp.zeros((x3.shape[0], 1), F32)
        for jj in range(nj):
            xj = xs_ref[jj]
            ss = ss + jnp.sum(xj * xj, axis=-1, keepdims=True)
        inv = lax.rsqrt(ss / d + EPS)
        logits = jnp.zeros(lg_ref.shape, F32)
        for jj in range(nj):
            cols = slice(jj * tn, (jj + 1) * tn)
            hj = (xs_ref[jj] * inv * ng_ref[:, cols]) * (1.0 + sc_ref[:, cols]) + sh_ref[:, cols]
            hi = hj.astype(BF16)
            lo = (hj - hi.astype(F32)).astype(BF16)
            h_ref[:, cols] = hi
            w = wr_ref[cols, :]
            whi = w.astype(BF16)
            wlo = (w - whi.astype(F32)).astype(BF16)
            logits = (logits + jnp.dot(hi, whi, preferred_element_type=F32)
                      + (jnp.dot(hi, wlo, preferred_element_type=F32)
                         + jnp.dot(lo, whi, preferred_element_type=F32)))
        lg_ref[...] = logits


def _oproj_layer(attn, x, gate, norm_g, sh, sc, w_o, router_w):
    bsz, seq, d = x.shape
    n_exp = router_w.shape[1]
    assert n_exp <= LANES
    wr = jnp.pad(router_w, ((0, 0), (0, LANES - n_exp)))
    tm = _tile(seq, 512)
    tn = _tile(d, 512)
    nj = d // tn
    per_b = seq // tm
    tok = bsz * seq
    vec_t = pl.BlockSpec((None, 1, tn), lambda i, j: (i // per_b, 0, j))
    vec = pl.BlockSpec((None, 1, d), lambda i, j: (i // per_b, 0, 0))
    x3, h, logits = pl.pallas_call(
        functools.partial(_oproj_kernel, nj=nj, tn=tn),
        grid=(bsz * per_b, nj),
        in_specs=[
            pl.BlockSpec((tm, d), lambda i, j: (i, 0)),
            pl.BlockSpec((d, tn), lambda i, j: (0, j)),
            pl.BlockSpec((tm, tn), lambda i, j: (i, j)),
            vec_t,
            pl.BlockSpec((1, d), lambda i, j: (0, 0)),
            vec, vec,
            pl.BlockSpec((d, LANES), lambda i, j: (0, 0)),
        ],
        out_specs=[
            pl.BlockSpec((tm, tn), lambda i, j: (i, j)),
            pl.BlockSpec((tm, d), lambda i, j: (i, 0)),
            pl.BlockSpec((tm, LANES), lambda i, j: (i, 0)),
        ],
        out_shape=[
            jax.ShapeDtypeStruct((tok, d), F32),
            jax.ShapeDtypeStruct((tok, d), BF16),
            jax.ShapeDtypeStruct((tok, LANES), F32),
        ],
        scratch_shapes=[pltpu.VMEM((nj, tm, tn), F32)],
        compiler_params=_params("parallel", "arbitrary"),
        name="oproj_prenorm_router",
    )(attn.reshape(tok, d), w_o.astype(BF16), x.reshape(tok, d), gate, norm_g.reshape(1, d),
      sh, sc, wr)
    return x3, h, logits


ROUTE_E0, ROUTE_E1, ROUTE_W0, ROUTE_W1, ROUTE_R0, ROUTE_R1 = range(6)


def _route_kernel(lg_ref, rb_ref, route_ref, cnt_ref, carry_ref, *, n_exp):
    step = pl.program_id(0)
    tr = lg_ref.shape[0]

    @pl.when(step == 0)
    def _():
        carry_ref[...] = jnp.zeros_like(carry_ref)

    lane = lax.broadcasted_iota(jnp.int32, (tr, LANES), 1)
    lanef = lane.astype(F32)
    logits = jnp.where(lane < n_exp, lg_ref[...] + rb_ref[...], -jnp.inf)
    m0 = jnp.max(logits, axis=1, keepdims=True)
    i0 = jnp.min(jnp.where(logits == m0, lanef, float(LANES)), axis=1, keepdims=True)
    oh0 = lanef == i0
    rest = jnp.where(oh0, -jnp.inf, logits)
    m1 = jnp.max(rest, axis=1, keepdims=True)
    i1 = jnp.min(jnp.where(rest == m1, lanef, float(LANES)), axis=1, keepdims=True)
    oh1 = lanef == i1
    e1 = jnp.exp(m1 - m0)
    w0 = 1.0 / (1.0 + e1)
    w1 = e1 / (1.0 + e1)
    oh = jnp.where(oh0 | oh1, 1.0, 0.0)
    r = lax.broadcasted_iota(jnp.int32, (tr, tr), 0)
    c = lax.broadcasted_iota(jnp.int32, (tr, tr), 1)
    earlier = jnp.where(c < r, 1.0, 0.0).astype(BF16)
    base = carry_ref[...] + jnp.dot(earlier, oh.astype(BF16), preferred_element_type=F32)
    r0 = jnp.sum(jnp.where(oh0, base, 0.0), axis=1, keepdims=True)
    r1 = jnp.sum(jnp.where(oh1, base, 0.0), axis=1, keepdims=True)
    total = carry_ref[...] + jnp.sum(oh, axis=0, keepdims=True)
    carry_ref[...] = total
    cnt_ref[...] = jnp.broadcast_to(total, cnt_ref.shape)
    out = jnp.zeros((tr, LANES), F32)
    for slot, val in ((ROUTE_E0, i0), (ROUTE_E1, i1), (ROUTE_W0, w0),
                      (ROUTE_W1, w1), (ROUTE_R0, r0), (ROUTE_R1, r1)):
        out = jnp.where(lane == slot, val, out)
    route_ref[...] = out


def _route(logits, router_b):
    tok = logits.shape[0]
    n_exp = router_b.shape[0]
    tr = _tile(tok, 512)
    rb = jnp.pad(router_b.astype(F32), (0, LANES - n_exp)).reshape(1, LANES)
    return pl.pallas_call(
        functools.partial(_route_kernel, n_exp=n_exp),
        grid=(tok // tr,),
        in_specs=[pl.BlockSpec((tr, LANES), lambda s: (s, 0)),
                  pl.BlockSpec((1, LANES), lambda s: (0, 0))],
        out_specs=[pl.BlockSpec((tr, LANES), lambda s: (s, 0)),
                   pl.BlockSpec((SUBLANES, LANES), lambda s: (0, 0))],
        out_shape=[jax.ShapeDtypeStruct((tok, LANES), F32),
                   jax.ShapeDtypeStruct((SUBLANES, LANES), F32)],
        scratch_shapes=[pltpu.VMEM((1, LANES), F32)],
        compiler_params=_params("arbitrary"),
        name="route_top2",
    )(logits, rb)


def _row_copy_kernel(start_ref, idx_ref, src_ref, *rest, td, gather):
    dst_ref, sem = rest[-2:]

    def copy(r, k):
        pos = start_ref[idx_ref[k, r]] + idx_ref[TOP_K + k, r]
        if gather:
            return pltpu.make_async_copy(src_ref.at[pos], dst_ref.at[k, r], sem)
        return pltpu.make_async_copy(src_ref.at[r], dst_ref.at[pos], sem)

    def issue(r, carry):
        for k in range(TOP_K):
            copy(r, k).start(priority=k % 2)
        return carry

    def drain(r, carry):
        for k in range(TOP_K):
            copy(r, k).wait()
        return carry

    lax.fori_loop(0, td, issue, 0)
    lax.fori_loop(0, td, drain, 0)


def _row_copy(src3, idx, row_start, scatter_into=None):
    n_steps, _, td = idx.shape
    slab = src3.shape[1:]
    gather = scatter_into is None
    any_spec = pl.BlockSpec(memory_space=pl.ANY)
    idx_spec = pl.BlockSpec((None, 2 * TOP_K, td), lambda s, start: (s, 0, 0),
                            memory_space=pltpu.SMEM)
    if gather:
        out_shape = jax.ShapeDtypeStruct((TOP_K, n_steps * td) + slab, src3.dtype)
        operands = [row_start, idx, src3]
        in_specs = [idx_spec, any_spec]
        out_spec = pl.BlockSpec((TOP_K, td) + slab, lambda s, start: (0, s, 0, 0))
        aliases = {}
    else:
        out_shape = jax.ShapeDtypeStruct(scatter_into.shape, src3.dtype)
        operands = [row_start, idx, src3, scatter_into]
        in_specs = [idx_spec, pl.BlockSpec((td,) + slab, lambda s, start: (s, 0, 0)), any_spec]
        out_spec = any_spec
        aliases = {3: 0}
    return pl.pallas_call(
        functools.partial(_row_copy_kernel, td=td, gather=gather),
        grid_spec=pltpu.PrefetchScalarGridSpec(
            num_scalar_prefetch=1,
            grid=(n_steps,),
            in_specs=in_specs,
            out_specs=out_spec,
            scratch_shapes=[pltpu.SemaphoreType.DMA(())],
        ),
        out_shape=out_shape,
        input_output_aliases=aliases,
        compiler_params=pltpu.CompilerParams(dimension_semantics=("arbitrary",),
                                             vmem_limit_bytes=VMEM_LIMIT,
                                             has_side_effects=True),
        name="moe_gather" if gather else "moe_dispatch",
    )(*operands)


def _moe_gateup_kernel(te_ref, na_ref, h_ref, w1_ref, w3_ref, o_ref):
    active = pl.program_id(0) < na_ref[0]

    @pl.when(active)
    def _():
        h = h_ref[...]
        a = jnp.dot(h, w1_ref[...], preferred_element_type=F32)
        b = jnp.dot(h, w3_ref[...], preferred_element_type=F32)
        o_ref[...] = _silu_mul(a, b).astype(BF16)

    @pl.when(jnp.logical_not(active))
    def _():
        o_ref[...] = jnp.zeros_like(o_ref)


def _moe_down_kernel(te_ref, na_ref, g_ref, w_ref, o_ref):
    active = pl.program_id(0) < na_ref[0]

    @pl.when(active)
    def _():
        o_ref[...] = jnp.dot(g_ref[...], w_ref[...], preferred_element_type=F32).astype(BF16)

    @pl.when(jnp.logical_not(active))
    def _():
        o_ref[...] = jnp.zeros_like(o_ref)


def _moe_ffn(hs, tile_expert, n_active, w1, w3, w2, *, tm):
    rows, d = hs.shape
    tn = _tile(w1.shape[-1], 512)
    w1, w3, w2 = _swiglu_weights(w1, w3, w2, tn)
    ff = w1.shape[-1]
    tn2 = _tile(d, 1024)
    n_tiles = rows // tm

    def w_map(i, j, te, na):
        return (te[i], 0, jnp.where(i < na[0], j, 0))

    w_up = pl.BlockSpec((None, d, tn), w_map)
    hidden = pl.pallas_call(
        _moe_gateup_kernel,
        grid_spec=pltpu.PrefetchScalarGridSpec(
            num_scalar_prefetch=2,
            grid=(n_tiles, ff // tn),
            in_specs=[pl.BlockSpec((tm, d), lambda i, j, te, na: (i, 0)), w_up, w_up],
            out_specs=pl.BlockSpec((tm, tn), lambda i, j, te, na: (i, j)),
        ),
        out_shape=jax.ShapeDtypeStruct((rows, ff), BF16),
        compiler_params=_params("parallel", "arbitrary"),
        name="moe_gate_up",
    )(tile_expert, n_active, hs, w1, w3)
    return pl.pallas_call(
        _moe_down_kernel,
        grid_spec=pltpu.PrefetchScalarGridSpec(
            num_scalar_prefetch=2,
            grid=(n_tiles, d // tn2),
            in_specs=[
                pl.BlockSpec((tm, ff), lambda i, j, te, na: (i, 0)),
                pl.BlockSpec((None, ff, tn2), w_map),
            ],
            out_specs=pl.BlockSpec((tm, tn2), lambda i, j, te, na: (i, j)),
        ),
        out_shape=jax.ShapeDtypeStruct((rows, d), BF16),
        compiler_params=_params("parallel", "arbitrary"),
        name="moe_down",
    )(tile_expert, n_active, hidden, w2)


def _combine_kernel(x_ref, y_ref, route_ref, gate_ref, o_ref):
    route = route_ref[...]
    lane = lax.broadcasted_iota(jnp.int32, route.shape, 1)
    w0 = jnp.sum(jnp.where(lane == ROUTE_W0, route, 0.0), axis=1, keepdims=True)
    w1 = jnp.sum(jnp.where(lane == ROUTE_W1, route, 0.0), axis=1, keepdims=True)
    y = w0 * y_ref[0].astype(F32) + w1 * y_ref[1].astype(F32)
    o_ref[...] = x_ref[...] + gate_ref[...] * y


def _combine(x3, y2, route, gate, *, seq):
    tok, d = x3.shape
    tm = _tile(seq, 256)
    per_b = seq // tm
    return pl.pallas_call(
        _combine_kernel,
        grid=(tok // tm,),
        in_specs=[
            pl.BlockSpec((tm, d), lambda i: (i, 0)),
            pl.BlockSpec((TOP_K, tm, d), lambda i: (0, i, 0)),
            pl.BlockSpec((tm, LANES), lambda i: (i, 0)),
            pl.BlockSpec((None, 1, d), lambda i: (i // per_b, 0, 0)),
        ],
        out_specs=pl.BlockSpec((tm, d), lambda i: (i, 0)),
        out_shape=jax.ShapeDtypeStruct((tok, d), F32),
        compiler_params=_params("parallel"),
        name="moe_combine",
    )(x3, y2, route, gate)


MOE_TILE = 512


def _moe_rows(tok, n_exp):
    return ((TOP_K * tok) // MOE_TILE + n_exp) * MOE_TILE


def _moe_layer(x3, h, logits, gate, router_b, w1, w3, w2, sorted_zeros, *, seq):
    tok, d = x3.shape
    n_exp = router_b.shape[0]
    slabs = d // LANES
    tm = MOE_TILE
    route, counts = _route(logits, router_b)

    cnt = counts[0, :n_exp].astype(jnp.int32)
    tiles_per = (cnt + tm - 1) // tm
    tile_end = jnp.cumsum(tiles_per)
    row_start = (tile_end - tiles_per) * tm
    n_tiles = _moe_rows(tok, n_exp) // tm
    tile_expert = jnp.minimum(
        jnp.searchsorted(tile_end, jnp.arange(n_tiles, dtype=jnp.int32), side="right"),
        n_exp - 1).astype(jnp.int32)
    n_active = tile_end[-1:].astype(jnp.int32)

    td = _tile(tok, 256)
    idx = jnp.concatenate([route[:, ROUTE_E0:ROUTE_E1 + 1], route[:, ROUTE_R0:ROUTE_R1 + 1]],
                          axis=1).astype(jnp.int32)
    idx = idx.reshape(tok // td, td, 2 * TOP_K).transpose(0, 2, 1)

    rows = n_tiles * tm
    hs3 = _row_copy(h.reshape(tok, slabs, LANES), idx, row_start, scatter_into=sorted_zeros)
    ys = _moe_ffn(hs3.reshape(rows, d), tile_expert, n_active, w1, w3, w2, tm=tm)
    y4 = _row_copy(ys.reshape(rows, slabs, LANES), idx, row_start)
    return _combine(x3, y4.reshape(TOP_K, tok, d), route, gate, seq=seq)


def kernel(x, c, ada_w, ada_b, norm_mix_g, norm_ffn_g, pool_w, pool_b, pool_scale, kv_ada_w,
           kv_ada_b, kv_norm_g, w_k, w_v, k_norm_g, w_q, q_norm_g, w_o, rel_bias, ffn_w1, ffn_w3,
           ffn_w2, router_w, router_b, moe_w1, moe_w3, moe_w2):
    bsz, seq, d = x.shape
    assert ada_w.shape[0] == 2 and pool_w.shape[0] == 1 and w_q.shape[0] == 1

    mod = _adaln(c, ada_w, ada_b)
    kv_mod = _adaln(c, kv_ada_w[None], kv_ada_b[None])[0]

    def vecs(m, n):
        return [m[:, None, k * d:(k + 1) * d] for k in range(n)]

    sh_m0, sc_m0, g_m0, sh_f0, sc_f0, g_f0 = vecs(mod[0], 6)
    sh_m1, sc_m1, g_m1, sh_f1, sc_f1, g_f1 = vecs(mod[1], 6)
    sh_kv, sc_kv = vecs(kv_mod, 2)

    x1 = _pool_layer(x, sh_m0, sc_m0, g_m0, norm_mix_g[0], pool_w[0], pool_b[0], pool_scale[0])
    x2 = _ffn_layer(x1, sh_f0, sc_f0, g_f0, norm_ffn_g[0], ffn_w1[0], ffn_w3[0], ffn_w2[0])

    kvq = _kvq_proj(x2, sh_kv, sc_kv, sh_m1, sc_m1, kv_norm_g, norm_mix_g[1], k_norm_g,
                    q_norm_g[0], w_k, w_v, w_q[0])
    sorted_shape = (_moe_rows(bsz * seq, router_b.shape[-1]), d // LANES, LANES)
    attn, (e_w1, e_w3, e_w2), sorted_zeros = _moba_attention(
        kvq, _bias_tables(rel_bias), (moe_w1[0], moe_w3[0], moe_w2[0]), sorted_shape)
    x3, h, logits = _oproj_layer(attn, x2, g_m1, norm_ffn_g[1], sh_f1, sc_f1, w_o[0], router_w[0])

    out = _moe_layer(x3, h, logits, g_f1, router_b[0], e_w1, e_w3, e_w2, sorted_zeros, seq=seq)
    return out.reshape(bsz, seq, d)
```
